```python
import jax, jax.numpy as jnp
from jax import lax
import numpy as np

D_MODEL = 2048
BATCH = 2
SEQ = 4096
DEPTH = 4
DEC_BATCH = 8
DEC_SEQ = 4
PAST_LEN = 16384
PAGE_SIZE = 128

HEAD_DIM = 128
RET_HEADS = 3 * D_MODEL // (8 * HEAD_DIM)
FOX_HEADS = 3 * D_MODEL // (8 * HEAD_DIM)
RET_W = RET_HEADS * HEAD_DIM
FOX_W = FOX_HEADS * HEAD_DIM
CONV_DIM = D_MODEL - RET_W - FOX_W
CONV_W = 31
RET_CHUNK = 128
FOX_QBLOCK = 128
ROPE_BASE = 10000.0
FF_DIM = 5504
N_EXPERTS = 8
TOP_K = 2
N_DENSE = (DEPTH + 1) // 2
N_MOE = DEPTH // 2
ALPHA = (2 * DEPTH) ** 0.25
BETA = (8 * DEPTH) ** -0.25
LN_EPS = 1e-5
NEG_INF = -1e30
FORGET_BIAS_INIT = 2.0
ADA_STD = 0.5
O_RQ = 0
O_RK = O_RQ + RET_W
O_RV = O_RK + RET_W
O_RG = O_RV + RET_W
O_CA = O_RG + RET_W
O_CB = O_CA + CONV_DIM
O_FQ = O_CB + CONV_DIM
O_FK = O_FQ + FOX_W
O_FV = O_FK + FOX_W
O_FF = O_FV + FOX_W
IN_COLS = O_FF + FOX_HEADS

kernel_name = "hybrid_ret_conv_fox_deepnorm_step"


def layer_norm(x, w, b):
    xf = x.astype(jnp.float32)
    mu = jnp.mean(xf, axis=-1, keepdims=True)
    var = jnp.mean(jnp.square(xf - mu), axis=-1, keepdims=True)
    return ((xf - mu) * lax.rsqrt(var + LN_EPS) * w + b).astype(x.dtype)


def rope(x, pos):
    half = HEAD_DIM // 2
    inv = ROPE_BASE ** (-jnp.arange(half, dtype=jnp.float32) / half)
    ang = pos.astype(jnp.float32)[:, None] * inv[None, :]
    cos = jnp.cos(ang)[None, :, None, :]
    sin = jnp.sin(ang)[None, :, None, :]
    x1 = x[..., :half].astype(jnp.float32)
    x2 = x[..., half:].astype(jnp.float32)
    return jnp.concatenate([x1 * cos - x2 * sin, x1 * sin + x2 * cos], axis=-1)


def retention(q, k, v, R0):
    B, T, H, d = q.shape
    C = min(T, RET_CHUNK)
    n = T // C
    lg = jnp.log(1.0 - 2.0 ** (-5.0 - jnp.arange(H, dtype=jnp.float32)))
    idx = jnp.arange(C, dtype=jnp.float32)
    diff = idx[:, None] - idx[None, :]
    dmat = jnp.where(diff[None] >= 0, jnp.exp(jnp.maximum(diff, 0.0)[None] * lg[:, None, None]), 0.0)
    xi = jnp.exp((idx[None, :] + 1.0) * lg[:, None]).T[None, :, :, None]
    zeta = jnp.exp((C - 1.0 - idx)[None, :] * lg[:, None])
    g_c = jnp.exp(C * lg)[None, :, None, None]

    def step(R, qkv):
        qc, kc, vc = qkv
        s = jnp.einsum('bihd,bjhd->bhij', qc, kc) * dmat
        y = jnp.einsum('bhij,bjhd->bihd', s, vc) + jnp.einsum('bihd,bhde->bihe', qc, R) * xi
        R = g_c * R + jnp.einsum('bjhd,bjhe,hj->bhde', kc, vc, zeta)
        return R, y

    def chunks(a):
        return a.reshape(B, n, C, H, d).swapaxes(0, 1)

    R, ys = lax.scan(step, R0, (chunks(q), chunks(k), chunks(v)))
    return ys.swapaxes(0, 1).reshape(B, T, H, d), R


def fox_block(q, cq, qpos, k, v, ck, kpos):
    s = jnp.einsum('bqhd,bkhd->bhqk', q, k).astype(jnp.float32) * HEAD_DIM ** -0.5
    s = s + jnp.transpose(cq, (0, 2, 1))[:, :, :, None] - jnp.transpose(ck, (0, 2, 1))[:, :, None, :]
    s = jnp.where(kpos[None, :] <= qpos[:, None], s, NEG_INF)
    p = jax.nn.softmax(s, axis=-1)
    return jnp.einsum('bhqk,bkhd->bqhd', p.astype(v.dtype), v)


def token_mixer(h, l, R0, conv_buf, k_past, v_past, logf_past, W):
    B, T, _ = h.shape
    P = k_past.shape[1]
    pos = P + jnp.arange(T)
    z = h @ W["w_in"][l]

    def heads(a):
        return a.reshape(B, T, -1, HEAD_DIM)

    rq = rope(heads(z[..., O_RQ:O_RK]), pos)
    rk = rope(heads(z[..., O_RK:O_RV]), pos) * HEAD_DIM ** -0.5
    rv = heads(z[..., O_RV:O_RG]).astype(jnp.float32)
    y_ret, R_new = retention(rq, rk, rv, R0.astype(jnp.float32))
    mu = jnp.mean(y_ret, axis=-1, keepdims=True)
    var = jnp.mean(jnp.square(y_ret - mu), axis=-1, keepdims=True)
    y_ret = ((y_ret - mu) * lax.rsqrt(var + LN_EPS)).reshape(B, T, RET_W) * W["ret_gn_w"][l]
    ret_out = y_ret * jax.nn.silu(z[..., O_RG:O_CA].astype(jnp.float32))

    u = z[..., O_CA:O_CB] * jax.nn.sigmoid(z[..., O_CB:O_FQ])
    u_full = jnp.concatenate([conv_buf.astype(u.dtype), u], axis=1)
    conv = lax.conv_general_dilated(u_full, W["conv_w"][l][:, None, :].astype(u.dtype),
                                    window_strides=(1,), padding="VALID",
                                    dimension_numbers=("NWC", "WIO", "NWC"),
                                    feature_group_count=CONV_DIM) + W["conv_b"][l]
    conv_out = jax.nn.silu(layer_norm(conv, W["conv_ln_w"][l], W["conv_ln_b"][l]))
    conv_new = u_full[:, -(CONV_W - 1):]

    fq = heads(z[..., O_FQ:O_FK])
    fk = heads(z[..., O_FK:O_FV])
    fv = heads(z[..., O_FV:O_FF])
    logf = jax.nn.log_sigmoid((z[..., O_FF:IN_COLS] + W["b_forget"][l]).astype(jnp.float32))
    k_all = jnp.concatenate([k_past.astype(fk.dtype), fk], axis=1)
    v_all = jnp.concatenate([v_past.astype(fv.dtype), fv], axis=1)
    c_all = jnp.cumsum(jnp.concatenate([logf_past.astype(jnp.float32), logf], axis=1), axis=1)
    kpos = jnp.arange(P + T)
    qb_len = min(T, FOX_QBLOCK)
    nb = T // qb_len
    q_blocks = fq.reshape(B, nb, qb_len, FOX_HEADS, HEAD_DIM).swapaxes(0, 1)
    cq_blocks = c_all[:, P:].reshape(B, nb, qb_len, FOX_HEADS).swapaxes(0, 1)
    pos_blocks = pos.reshape(nb, qb_len)
    y_blocks = lax.map(lambda blk: fox_block(blk[0], blk[1], blk[2], k_all, v_all, c_all, kpos),
                       (q_blocks, cq_blocks, pos_blocks))
    fox_out = y_blocks.swapaxes(0, 1).reshape(B, T, FOX_W)

    mixed = jnp.concatenate([ret_out.astype(h.dtype), conv_out.astype(h.dtype), fox_out.astype(h.dtype)], axis=-1)
    return mixed @ W["w_out"][l], (R_new, conv_new, fk, fv, logf)


def dense_ffn(h, i, W):
    a = jax.nn.silu(h @ W["ffn_w_gate"][i]) * (h @ W["ffn_w_up"][i])
    return a @ W["ffn_w_down"][i]


def moe_ffn(h, i, W):
    logits = (h @ W["moe_router"][i]).astype(jnp.float32) + W["moe_router_b"][i]
    top_val, top_idx = lax.top_k(logits, TOP_K)
    probs = jax.nn.softmax(top_val, axis=-1)
    gates = jnp.einsum('btk,btke->bte', probs, jax.nn.one_hot(top_idx, N_EXPERTS, dtype=jnp.float32))
    out = jnp.zeros(h.shape, jnp.float32)
    for e in range(N_EXPERTS):
        a = jax.nn.silu(h @ W["moe_w_gate"][i, e]) * (h @ W["moe_w_up"][i, e])
        out = out + gates[..., e:e + 1] * (a @ W["moe_w_down"][i, e])
    return out.astype(h.dtype)


def run_trunk(x, c, past_fn, W):
    cs = jax.nn.silu(c)
    rets, convs, ks, vs, lfs = [], [], [], [], []
    for l in range(DEPTH):
        mod = cs @ W["w_ada"][l] + W["b_ada"][l]
        sh1, sc1, g1, sh2, sc2, g2 = jnp.split(mod[:, None, :], 6, axis=-1)
        h = x * (1.0 + sc1) + sh1
        m, (R, cb, k, v, lf) = token_mixer(h, l, *past_fn(l), W)
        x = layer_norm(ALPHA * x + g1 * m, W["ln1_w"][l], W["ln1_b"][l])
        h = x * (1.0 + sc2) + sh2
        f = dense_ffn(h, l // 2, W) if l % 2 == 0 else moe_ffn(h, l // 2, W)
        x = layer_norm(ALPHA * x + g2 * f, W["ln2_w"][l], W["ln2_b"][l])
        rets.append(R); convs.append(cb); ks.append(k); vs.append(v); lfs.append(lf)
    return (x, jnp.stack(rets, 1), jnp.stack(convs, 1), jnp.stack(ks, 1),
            jnp.stack(vs, 1), jnp.stack(lfs, 1))


def setup_inputs(seed: int = 0) -> dict:
    key = jax.random.key(seed)
    ks = jax.random.split(key, 40)
    f32 = jnp.float32

    def nrm(k, shape, s):
        return jax.random.normal(k, shape, f32) * s

    n_pages = PAST_LEN // PAGE_SIZE
    n_used = DEC_BATCH * n_pages
    n_pool = n_used + n_used // 4
    perm = jax.random.permutation(ks[0], n_pool)
    page_table = perm[:n_used].reshape(DEC_BATCH, n_pages).astype(jnp.int32)

    col_scale = jnp.ones((IN_COLS,), f32).at[O_RV:O_RG].set(BETA).at[O_FV:O_FF].set(BETA)
    return {
        "x_prompt": nrm(ks[1], (BATCH, SEQ, D_MODEL), 1.0),
        "x_sample": nrm(ks[2], (DEC_BATCH, DEC_SEQ, D_MODEL), 1.0),
        "cache_k": nrm(ks[3], (n_pool, DEPTH, PAGE_SIZE, FOX_HEADS, HEAD_DIM), 1.0),
        "cache_v": nrm(ks[4], (n_pool, DEPTH, PAGE_SIZE, FOX_HEADS, HEAD_DIM), BETA),
        "cache_logf": jax.nn.log_sigmoid(FORGET_BIAS_INIT + nrm(ks[5], (n_pool, DEPTH, PAGE_SIZE, FOX_HEADS), 1.0)),
        "state_ret": nrm(ks[6], (DEC_BATCH, DEPTH, RET_HEADS, HEAD_DIM, HEAD_DIM), HEAD_DIM ** -0.5),
        "state_conv": nrm(ks[7], (DEC_BATCH, DEPTH, CONV_W - 1, CONV_DIM), 0.5),
        "page_table": page_table,
        "c_prompt": nrm(ks[8], (BATCH, D_MODEL), 1.0),
        "c_sample": nrm(ks[9], (DEC_BATCH, D_MODEL), 1.0),
        "w_ada": nrm(ks[10], (DEPTH, D_MODEL, 6 * D_MODEL), ADA_STD * D_MODEL ** -0.5),
        "b_ada": nrm(ks[11], (DEPTH, 6 * D_MODEL), 0.02),
        "w_in": nrm(ks[12], (DEPTH, D_MODEL, IN_COLS), D_MODEL ** -0.5) * col_scale,
        "b_forget": FORGET_BIAS_INIT + nrm(ks[13], (DEPTH, FOX_HEADS), 0.1),
        "ret_gn_w": 1.0 + nrm(ks[14], (DEPTH, RET_W), 0.02),
        "conv_w": nrm(ks[15], (DEPTH, CONV_W, CONV_DIM), CONV_W ** -0.5),
        "conv_b": nrm(ks[16], (DEPTH, CONV_DIM), 0.02),
        "conv_ln_w": 1.0 + nrm(ks[17], (DEPTH, CONV_DIM), 0.02),
        "conv_ln_b": nrm(ks[18], (DEPTH, CONV_DIM), 0.02),
        "w_out": nrm(ks[19], (DEPTH, D_MODEL, D_MODEL), BETA * D_MODEL ** -0.5),
        "ln1_w": 1.0 + nrm(ks[20], (DEPTH, D_MODEL), 0.02),
        "ln1_b": nrm(ks[21], (DEPTH, D_MODEL), 0.02),
        "ln2_w": 1.0 + nrm(ks[22], (DEPTH, D_MODEL), 0.02),
        "ln2_b": nrm(ks[23], (DEPTH, D_MODEL), 0.02),
        "ffn_w_gate": nrm(ks[24], (N_DENSE, D_MODEL, FF_DIM), D_MODEL ** -0.5),
        "ffn_w_up": nrm(ks[25], (N_DENSE, D_MODEL, FF_DIM), D_MODEL ** -0.5),
        "ffn_w_down": nrm(ks[26], (N_DENSE, FF_DIM, D_MODEL), BETA * FF_DIM ** -0.5),
        "moe_router": nrm(ks[27], (N_MOE, D_MODEL, N_EXPERTS), D_MODEL ** -0.5),
        "moe_router_b": nrm(ks[28], (N_MOE, N_EXPERTS), 0.01),
        "moe_w_gate": nrm(ks[29], (N_MOE, N_EXPERTS, D_MODEL, FF_DIM), D_MODEL ** -0.5),
        "moe_w_up": nrm(ks[30], (N_MOE, N_EXPERTS, D_MODEL, FF_DIM), D_MODEL ** -0.5),
        "moe_w_down": nrm(ks[31], (N_MOE, N_EXPERTS, FF_DIM, D_MODEL), BETA * FF_DIM ** -0.5),
    }


def reference(x_prompt, x_sample, cache_k, cache_v, cache_logf, state_ret, state_conv, page_table,
              c_prompt, c_sample, w_ada, b_ada, w_in, b_forget, ret_gn_w, conv_w, conv_b,
              conv_ln_w, conv_ln_b, w_out, ln1_w, ln1_b, ln2_w, ln2_b, ffn_w_gate, ffn_w_up,
              ffn_w_down, moe_router, moe_router_b, moe_w_gate, moe_w_up, moe_w_down):
    W = dict(w_ada=w_ada, b_ada=b_ada, w_in=w_in, b_forget=b_forget, ret_gn_w=ret_gn_w,
             conv_w=conv_w, conv_b=conv_b, conv_ln_w=conv_ln_w, conv_ln_b=conv_ln_b, w_out=w_out,
             ln1_w=ln1_w, ln1_b=ln1_b, ln2_w=ln2_w, ln2_b=ln2_b, ffn_w_gate=ffn_w_gate,
             ffn_w_up=ffn_w_up, ffn_w_down=ffn_w_down, moe_router=moe_router,
             moe_router_b=moe_router_b, moe_w_gate=moe_w_gate, moe_w_up=moe_w_up,
             moe_w_down=moe_w_down)
    bp = x_prompt.shape[0]
    bs = x_sample.shape[0]
    n_pages = page_table.shape[1]
    past_rows = n_pages * PAGE_SIZE

    def prompt_past(l):
        return (jnp.zeros((bp, RET_HEADS, HEAD_DIM, HEAD_DIM), jnp.float32),
                jnp.zeros((bp, CONV_W - 1, CONV_DIM), x_prompt.dtype),
                jnp.zeros((bp, 0, FOX_HEADS, HEAD_DIM), x_prompt.dtype),
                jnp.zeros((bp, 0, FOX_HEADS, HEAD_DIM), x_prompt.dtype),
                jnp.zeros((bp, 0, FOX_HEADS), jnp.float32))

    def sample_past(l):
        kp = cache_k[page_table, l].reshape(bs, past_rows, FOX_HEADS, HEAD_DIM)
        vp = cache_v[page_table, l].reshape(bs, past_rows, FOX_HEADS, HEAD_DIM)
        lp = cache_logf[page_table, l].reshape(bs, past_rows, FOX_HEADS)
        return (state_ret[:, l], state_conv[:, l], kp, vp, lp)

    y_prompt, ret_p, conv_p, k_p, v_p, lf_p = run_trunk(x_prompt, c_prompt, prompt_past, W)
    y_sample, ret_s, conv_s, k_s, v_s, lf_s = run_trunk(x_sample, c_sample, sample_past, W)
    return (y_prompt, y_sample, ret_p, ret_s, conv_p, conv_s, k_p, k_s, v_p, v_s, lf_p, lf_s)
```

```python
import functools

import jax
import jax.numpy as jnp
from jax import lax
from jax.experimental import pallas as pl
from jax.experimental.pallas import tpu as pltpu

F32 = jnp.float32
BF16 = jnp.bfloat16

LANES = 128
SUBLANES = 8
VMEM_LIMIT_BYTES = 56 * 1024 * 1024

HEAD_DIM = 128
CONV_W = 31
CONV_HALO = 32
RET_CHUNK = 128
TOP_K = 2
LN_EPS = 1e-5
NEG_INF = -1e30
ROPE_BASE = 10000.0


def _cparams(sem, vmem=VMEM_LIMIT_BYTES):
    return pltpu.CompilerParams(dimension_semantics=sem, vmem_limit_bytes=vmem)


def _silu(x):
    return x * jax.nn.sigmoid(x)


def _dot(a, b):
    return jnp.dot(a, b, preferred_element_type=F32)


def _dot_nt(a, b):
    return lax.dot_general(a, b, (((1,), (1,)), ((), ())), preferred_element_type=F32)


def _split3(x):
    hi = x.astype(BF16)
    r1 = x - hi.astype(F32)
    mid = r1.astype(BF16)
    lo = (r1 - mid.astype(F32)).astype(BF16)
    return hi, mid, lo


def _log_sigmoid(x):
    return -(jnp.maximum(-x, 0.0) + jnp.log1p(jnp.exp(-jnp.abs(x))))


def _layer_norm_rows(y, w, b):
    mu = jnp.mean(y, axis=-1, keepdims=True)
    d = y - mu
    var = jnp.mean(d * d, axis=-1, keepdims=True)
    return d * lax.rsqrt(var + LN_EPS) * w + b


def _ada_kernel(c_ref, w_ref, b_ref, o_ref):
    cs = _silu(c_ref[...]).astype(BF16)
    o_ref[...] = _dot(cs, w_ref[...].astype(BF16)) + b_ref[...]


def ada_modulation(c_rows, w_ada, b_ada, tn=1024):
    depth, d, n6 = w_ada.shape
    r = c_rows.shape[0]
    return pl.pallas_call(
        _ada_kernel,
        out_shape=jax.ShapeDtypeStruct((depth, r, n6), F32),
        grid=(depth, n6 // tn),
        in_specs=[
            pl.BlockSpec((r, d), lambda l, j: (0, 0)),
            pl.BlockSpec((None, d, tn), lambda l, j: (l, 0, j)),
            pl.BlockSpec((None, 1, tn), lambda l, j: (l, 0, j)),
        ],
        out_specs=pl.BlockSpec((None, r, tn), lambda l, j: (l, 0, j)),
        compiler_params=_cparams(("arbitrary", "arbitrary")),
        name="ada_modulation",
    )(c_rows, w_ada, b_ada.reshape(depth, 1, n6))


def _mod_spec(arr, tm, rows_per_seq, grid_rank):
    d = arr.shape[-1]
    if arr.ndim == 3:
        assert rows_per_seq % tm == 0
        per = rows_per_seq // tm
        if grid_rank == 1:
            return pl.BlockSpec((None, 1, d), lambda i: (i // per, 0, 0))
        return pl.BlockSpec((None, 1, d), lambda i, j: (i // per, 0, 0))
    if grid_rank == 1:
        return pl.BlockSpec((tm, d), lambda i: (i, 0))
    return pl.BlockSpec((tm, d), lambda i, j: (i, 0))


def _proj_in_kernel(x_ref, sc_ref, sh_ref, w_ref, wf_ref, z_ref, zf_ref, xb_ref):
    @pl.when(pl.program_id(1) == 0)
    def _():
        h = x_ref[...] * (1.0 + sc_ref[...]) + sh_ref[...]
        xb_ref[...] = h.astype(BF16)
        zf_ref[...] = _dot(xb_ref[...], wf_ref[...].astype(BF16))

    z_ref[...] = _dot(xb_ref[...], w_ref[...].astype(BF16))


def proj_in(x, sc, sh, w_in, w_forget, layer, n_main, tm, rows_per_seq, tn=512):
    n, d = x.shape
    return pl.pallas_call(
        _proj_in_kernel,
        out_shape=(jax.ShapeDtypeStruct((n, n_main), F32), jax.ShapeDtypeStruct((n, LANES), F32)),
        grid=(n // tm, n_main // tn),
        in_specs=[
            pl.BlockSpec((tm, d), lambda i, j: (i, 0)),
            _mod_spec(sc, tm, rows_per_seq, 2),
            _mod_spec(sh, tm, rows_per_seq, 2),
            pl.BlockSpec((None, d, tn), lambda i, j: (layer, 0, j)),
            pl.BlockSpec((None, d, LANES), lambda i, j: (layer, 0, 0)),
        ],
        out_specs=(pl.BlockSpec((tm, tn), lambda i, j: (i, j)),
                   pl.BlockSpec((tm, LANES), lambda i, j: (i, 0))),
        scratch_shapes=[pltpu.VMEM((tm, d), BF16)],
        compiler_params=_cparams(("arbitrary", "arbitrary")),
        name="proj_in",
    )(x, sc, sh, w_in, w_forget)


def _proj_out_kernel(a_ref, b_ref, c_ref, w_ref, o_ref, xb_ref):
    @pl.when(pl.program_id(1) == 0)
    def _():
        wa, wb = a_ref.shape[1], b_ref.shape[1]
        xb_ref[:, 0:wa] = a_ref[...].astype(BF16)
        xb_ref[:, wa:wa + wb] = b_ref[...].astype(BF16)
        xb_ref[:, wa + wb:] = c_ref[...].astype(BF16)

    o_ref[...] = _dot(xb_ref[...], w_ref[...].astype(BF16))


def proj_out(a, b, c, w_out, layer, tm, tn=512):
    n = a.shape[0]
    d = w_out.shape[1]
    dout = w_out.shape[2]
    return pl.pallas_call(
        _proj_out_kernel,
        out_shape=jax.ShapeDtypeStruct((n, dout), F32),
        grid=(n // tm, dout // tn),
        in_specs=[
            pl.BlockSpec((tm, a.shape[1]), lambda i, j: (i, 0)),
            pl.BlockSpec((tm, b.shape[1]), lambda i, j: (i, 0)),
            pl.BlockSpec((tm, c.shape[1]), lambda i, j: (i, 0)),
            pl.BlockSpec((None, d, tn), lambda i, j: (layer, 0, j)),
        ],
        out_specs=pl.BlockSpec((tm, tn), lambda i, j: (i, j)),
        scratch_shapes=[pltpu.VMEM((tm, d), BF16)],
        compiler_params=_cparams(("arbitrary", "arbitrary")),
        name="proj_out",
    )(a, b, c, w_out)


def _ln_res_kernel(x_ref, m_ref, g_ref, w_ref, b_ref, o_ref, *, alpha):
    y = alpha * x_ref[...] + g_ref[...] * m_ref[...]
    o_ref[...] = _layer_norm_rows(y, w_ref[...], b_ref[...])


def ln_residual(x, m, g, w, b, alpha, tm, rows_per_seq):
    n, d = x.shape
    return pl.pallas_call(
        functools.partial(_ln_res_kernel, alpha=alpha),
        out_shape=jax.ShapeDtypeStruct((n, d), F32),
        grid=(n // tm,),
        in_specs=[
            pl.BlockSpec((tm, d), lambda i: (i, 0)),
            pl.BlockSpec((tm, d), lambda i: (i, 0)),
            _mod_spec(g, tm, rows_per_seq, 1),
            pl.BlockSpec((1, d), lambda i: (0, 0)),
            pl.BlockSpec((1, d), lambda i: (0, 0)),
        ],
        out_specs=pl.BlockSpec((tm, d), lambda i: (i, 0)),
        compiler_params=_cparams(("arbitrary",)),
        name="ln_residual",
    )(x, m, g, w.reshape(1, d), b.reshape(1, d))


def _ffn_up_kernel(x_ref, sc_ref, sh_ref, wg_ref, wu_ref, a_ref, xb_ref):
    @pl.when(pl.program_id(1) == 0)
    def _():
        h = x_ref[...] * (1.0 + sc_ref[...]) + sh_ref[...]
        xb_ref[...] = h.astype(BF16)

    xb = xb_ref[...]
    gate = _dot(xb, wg_ref[...].astype(BF16))
    up = _dot(xb, wu_ref[...].astype(BF16))
    a_ref[...] = (_silu(gate) * up).astype(a_ref.dtype)


def ffn_up(x, sc, sh, w_gate, w_up, idx, tm, rows_per_seq, tf=512):
    n, d = x.shape
    ff = w_gate.shape[2]
    return pl.pallas_call(
        _ffn_up_kernel,
        out_shape=jax.ShapeDtypeStruct((n, ff), BF16),
        grid=(n // tm, pl.cdiv(ff, tf)),
        in_specs=[
            pl.BlockSpec((tm, d), lambda i, j: (i, 0)),
            _mod_spec(sc, tm, rows_per_seq, 2),
            _mod_spec(sh, tm, rows_per_seq, 2),
            pl.BlockSpec((None, d, tf), lambda i, j: (idx, 0, j)),
            pl.BlockSpec((None, d, tf), lambda i, j: (idx, 0, j)),
        ],
        out_specs=pl.BlockSpec((tm, tf), lambda i, j: (i, j)),
        scratch_shapes=[pltpu.VMEM((tm, d), BF16)],
        compiler_params=_cparams(("arbitrary", "arbitrary")),
        name="ffn_up",
    )(x, sc, sh, w_gate, w_up)


def _ffn_down_kernel(a_ref, w_ref, o_ref):
    o_ref[...] = _dot(a_ref[...], w_ref[...].astype(BF16))


def ffn_down(a, w_down, idx, tm, tn=256):
    n, ff = a.shape
    d = w_down.shape[2]
    return pl.pallas_call(
        _ffn_down_kernel,
        out_shape=jax.ShapeDtypeStruct((n, d), F32),
        grid=(n // tm, d // tn),
        in_specs=[
            pl.BlockSpec((tm, ff), lambda i, j: (i, 0)),
            pl.BlockSpec((None, ff, tn), lambda i, j: (idx, 0, j)),
        ],
        out_specs=pl.BlockSpec((tm, tn), lambda i, j: (i, j)),
        compiler_params=_cparams(("arbitrary", "arbitrary")),
        name="ffn_down",
    )(a, w_down)


def _retention_kernel(q_ref, k_ref, v_ref, g_ref, cos_ref, sin_ref, dmat_ref, xi_ref, zeta_ref,
                      gc_ref, gnw_ref, r0_ref, y_ref, rn_ref, r_scr, *, n_chunks):
    c = pl.program_id(2)

    @pl.when(c == 0)
    def _():
        r_scr[...] = r0_ref[...]

    cos = cos_ref[...]
    sin = sin_ref[...]

    def rope(x):
        return x * cos + pltpu.roll(x, HEAD_DIM // 2, axis=1) * sin

    q = rope(q_ref[...])
    k = rope(k_ref[...]) * (HEAD_DIM ** -0.5)
    qb = q.astype(BF16)
    kb = k.astype(BF16)
    vb = v_ref[...].astype(BF16)
    r_prev = r_scr[...]

    s = _dot_nt(qb, kb) * dmat_ref[...]
    y = _dot(s.astype(BF16), vb) + _dot(qb, r_prev.astype(BF16)) * xi_ref[...]
    kz_t = jnp.transpose(k * zeta_ref[...]).astype(BF16)
    r_new = gc_ref[...] * r_prev + _dot(kz_t, vb)
    r_scr[...] = r_new

    mu = jnp.mean(y, axis=-1, keepdims=True)
    d = y - mu
    var = jnp.mean(d * d, axis=-1, keepdims=True)
    yn = d * lax.rsqrt(var + LN_EPS) * gnw_ref[...]
    y_ref[...] = yn * _silu(g_ref[...])

    @pl.when(c == n_chunks - 1)
    def _():
        rn_ref[...] = r_new


def retention(z, tabs, gn_w, r0, n_heads, seq_len):
    n = z.shape[0]
    b = n // seq_len
    cch = RET_CHUNK
    nc = seq_len // cch
    hd = HEAD_DIM
    h_ = n_heads
    row = lambda bi, hi, ci: bi * nc + ci
    return pl.pallas_call(
        functools.partial(_retention_kernel, n_chunks=nc),
        out_shape=(jax.ShapeDtypeStruct((n, h_ * hd), F32), jax.ShapeDtypeStruct((b, h_, hd, hd), F32)),
        grid=(b, h_, nc),
        in_specs=[
            pl.BlockSpec((cch, hd), lambda bi, hi, ci: (row(bi, hi, ci), hi)),
            pl.BlockSpec((cch, hd), lambda bi, hi, ci: (row(bi, hi, ci), h_ + hi)),
            pl.BlockSpec((cch, hd), lambda bi, hi, ci: (row(bi, hi, ci), 2 * h_ + hi)),
            pl.BlockSpec((cch, hd), lambda bi, hi, ci: (row(bi, hi, ci), 3 * h_ + hi)),
            pl.BlockSpec((cch, hd), lambda bi, hi, ci: (ci, 0)),
            pl.BlockSpec((cch, hd), lambda bi, hi, ci: (ci, 0)),
            pl.BlockSpec((None, cch, cch), lambda bi, hi, ci: (hi, 0, 0)),
            pl.BlockSpec((None, cch, hd), lambda bi, hi, ci: (hi, 0, 0)),
            pl.BlockSpec((None, cch, hd), lambda bi, hi, ci: (hi, 0, 0)),
            pl.BlockSpec((None, 1, hd), lambda bi, hi, ci: (hi, 0, 0)),
            pl.BlockSpec((1, hd), lambda bi, hi, ci: (0, hi)),
            pl.BlockSpec((None, None, hd, hd), lambda bi, hi, ci: (bi, hi, 0, 0)),
        ],
        out_specs=(pl.BlockSpec((cch, hd), lambda bi, hi, ci: (row(bi, hi, ci), hi)),
                   pl.BlockSpec((None, None, hd, hd), lambda bi, hi, ci: (bi, hi, 0, 0))),
        scratch_shapes=[pltpu.VMEM((hd, hd), F32)],
        compiler_params=_cparams(("arbitrary", "arbitrary", "arbitrary")),
        name="retention",
    )(z, z, z, z, tabs["cos"], tabs["sin"], tabs["dmat"], tabs["xi"], tabs["zeta"], tabs["gc"],
      gn_w.reshape(1, h_ * hd), r0)


def retention_tables(n_heads, t_valid, pos0, t_pad):
    cch = min(t_valid, RET_CHUNK)
    half = HEAD_DIM // 2
    lg = jnp.log(1.0 - 2.0 ** (-5.0 - jnp.arange(n_heads, dtype=F32)))
    idx = jnp.arange(cch, dtype=F32)
    diff = idx[:, None] - idx[None, :]
    dmat = jnp.where(diff[None] >= 0, jnp.exp(jnp.maximum(diff, 0.0)[None] * lg[:, None, None]), 0.0)
    xi = jnp.exp((idx[None, :] + 1.0) * lg[:, None])
    zeta = jnp.exp((cch - 1.0 - idx)[None, :] * lg[:, None])
    gc = jnp.exp(cch * lg)
    pc = RET_CHUNK - cch
    dmat = jnp.pad(dmat, ((0, 0), (0, pc), (0, pc)))
    xi = jnp.broadcast_to(jnp.pad(xi, ((0, 0), (0, pc)))[:, :, None], (n_heads, RET_CHUNK, HEAD_DIM))
    zeta = jnp.broadcast_to(jnp.pad(zeta, ((0, 0), (0, pc)))[:, :, None], (n_heads, RET_CHUNK, HEAD_DIM))
    gc = jnp.broadcast_to(gc[:, None, None], (n_heads, 1, HEAD_DIM))
    pos = (pos0 + jnp.arange(t_pad)).astype(F32)
    inv = ROPE_BASE ** (-jnp.arange(half, dtype=F32) / half)
    ang = pos[:, None] * inv[None, :]
    cos = jnp.cos(ang)
    sin = jnp.sin(ang)
    return dict(dmat=dmat, xi=xi, zeta=zeta, gc=gc,
                cos=jnp.concatenate([cos, cos], axis=1), sin=jnp.concatenate([-sin, sin], axis=1))


CONV_ROWS = 32


def _conv_kernel(za_ref, zb_ref, buf_ref, cw_ref, cb_ref, lw_ref, lb_ref, o_ref, new_ref, u_scr,
                 *, tt, t_last, n_tiles):
    t = pl.program_id(1)

    @pl.when(t == 0)
    def _():
        u_scr[0:CONV_HALO, :] = buf_ref[...]

    @pl.when(t > 0)
    def _():
        u_scr[0:CONV_HALO, :] = u_scr[tt:tt + CONV_HALO, :]

    u_scr[CONV_HALO:CONV_HALO + tt, :] = za_ref[...] * jax.nn.sigmoid(zb_ref[...])

    off = CONV_HALO - (CONV_W - 1)
    for r0 in range(0, tt, CONV_ROWS):
        acc = u_scr[r0 + off:r0 + off + CONV_ROWS, :] * cw_ref[0:1, :]
        for w in range(1, CONV_W):
            acc = acc + u_scr[r0 + off + w:r0 + off + w + CONV_ROWS, :] * cw_ref[w:w + 1, :]
        y = _layer_norm_rows(acc + cb_ref[...], lw_ref[...], lb_ref[...])
        o_ref[r0:r0 + CONV_ROWS, :] = _silu(y)

    @pl.when(t == n_tiles - 1)
    def _():
        new_ref[...] = u_scr[t_last:t_last + CONV_HALO, :]


def conv_branch(z, col_a, col_b, conv_buf, cw, cb, lw, lb, seq_len, t_last, tt=128):
    n = z.shape[0]
    b = n // seq_len
    nt = seq_len // tt
    cdim = cw.shape[1]
    ja, jb = col_a // cdim, col_b // cdim
    return pl.pallas_call(
        functools.partial(_conv_kernel, tt=tt, t_last=t_last, n_tiles=nt),
        out_shape=(jax.ShapeDtypeStruct((n, cdim), F32), jax.ShapeDtypeStruct((b, CONV_HALO, cdim), F32)),
        grid=(b, nt),
        in_specs=[
            pl.BlockSpec((tt, cdim), lambda bi, ti: (bi * nt + ti, ja)),
            pl.BlockSpec((tt, cdim), lambda bi, ti: (bi * nt + ti, jb)),
            pl.BlockSpec((None, CONV_HALO, cdim), lambda bi, ti: (bi, 0, 0)),
            pl.BlockSpec((CONV_HALO, cdim), lambda bi, ti: (0, 0)),
            pl.BlockSpec((1, cdim), lambda bi, ti: (0, 0)),
            pl.BlockSpec((1, cdim), lambda bi, ti: (0, 0)),
            pl.BlockSpec((1, cdim), lambda bi, ti: (0, 0)),
        ],
        out_specs=(pl.BlockSpec((tt, cdim), lambda bi, ti: (bi * nt + ti, 0)),
                   pl.BlockSpec((None, CONV_HALO, cdim), lambda bi, ti: (bi, 0, 0))),
        scratch_shapes=[pltpu.VMEM((CONV_HALO + tt, cdim), F32)],
        compiler_params=_cparams(("arbitrary", "arbitrary")),
        name="conv_branch",
    )(z, z, conv_buf, cw, cb.reshape(1, cdim), lw.reshape(1, cdim), lb.reshape(1, cdim))


def _tri_cumsum(tri_ref, x):
    hi, mid, lo = _split3(x)
    tri = tri_ref[...]
    return _dot(tri, hi) + _dot(tri, mid) + _dot(tri, lo)


def _logf_kernel(zf_ref, bf_ref, tri_ref, lf_ref, c_ref, ct_ref, carry_ref):
    @pl.when(pl.program_id(1) == 0)
    def _():
        carry_ref[...] = jnp.zeros_like(carry_ref)

    lf = _log_sigmoid(zf_ref[...] + bf_ref[...])
    c = _tri_cumsum(tri_ref, lf) + carry_ref[...]
    carry_ref[...] = c[LANES - 1:LANES, :]
    lf_ref[...] = lf
    c_ref[...] = c
    ct_ref[...] = jnp.transpose(c)[0:SUBLANES, :]


def forget_cumsum(zf, b_forget_row, tri, seq_len):
    n = zf.shape[0]
    b = n // seq_len
    nt = seq_len // LANES
    return pl.pallas_call(
        _logf_kernel,
        out_shape=(jax.ShapeDtypeStruct((n, LANES), F32), jax.ShapeDtypeStruct((n, LANES), F32),
                   jax.ShapeDtypeStruct((b, SUBLANES, seq_len), F32)),
        grid=(b, nt),
        in_specs=[
            pl.BlockSpec((LANES, LANES), lambda bi, ti: (bi * nt + ti, 0)),
            pl.BlockSpec((1, LANES), lambda bi, ti: (0, 0)),
            pl.BlockSpec((LANES, LANES), lambda bi, ti: (0, 0)),
        ],
        out_specs=(pl.BlockSpec((LANES, LANES), lambda bi, ti: (bi * nt + ti, 0)),
                   pl.BlockSpec((LANES, LANES), lambda bi, ti: (bi * nt + ti, 0)),
                   pl.BlockSpec((None, SUBLANES, LANES), lambda bi, ti: (bi, 0, ti))),
        scratch_shapes=[pltpu.VMEM((1, LANES), F32)],
        compiler_params=_cparams(("arbitrary", "arbitrary")),
        name="forget_cumsum",
    )(zf, b_forget_row, tri)


def _fox_kernel(q_ref, k_ref, v_ref, c_ref, ct_ref, o_ref, m_scr, l_scr, acc_scr, cq_scr, *, tq, tk):
    h = pl.program_id(1)
    qi = pl.program_id(2)
    ki = pl.program_id(3)

    @pl.when(ki == 0)
    def _():
        m_scr[...] = jnp.full_like(m_scr, NEG_INF)
        l_scr[...] = jnp.zeros_like(l_scr)
        acc_scr[...] = jnp.zeros_like(acc_scr)
        lane = lax.broadcasted_iota(jnp.int32, (tq, LANES), 1)
        cq = jnp.sum(jnp.where(lane == h, c_ref[...], 0.0), axis=1, keepdims=True)
        cq_scr[...] = jnp.broadcast_to(cq, (tq, LANES))

    def update(masked):
        s = _dot_nt(q_ref[...].astype(BF16), k_ref[...].astype(BF16)) * (HEAD_DIM ** -0.5)
        s = s + (cq_scr[:, 0:1] - ct_ref[pl.ds(h, 1), :])
        if masked:
            row = lax.broadcasted_iota(jnp.int32, (tq, tk), 0)
            col = lax.broadcasted_iota(jnp.int32, (tq, tk), 1)
            s = jnp.where(col <= row, s, NEG_INF)
        m_prev = m_scr[...]
        m_new = jnp.maximum(m_prev, jnp.max(s, axis=1, keepdims=True))
        alpha = jnp.exp(m_prev - m_new)
        p = jnp.exp(s - m_new[:, 0:1])
        l_scr[...] = alpha * l_scr[...] + jnp.sum(p, axis=1, keepdims=True)
        acc_scr[...] = alpha * acc_scr[...] + _dot(p.astype(BF16), v_ref[...].astype(BF16))
        m_scr[...] = m_new

    @pl.when(ki < qi)
    def _():
        update(False)

    @pl.when(ki == qi)
    def _():
        update(True)
        o_ref[...] = acc_scr[...] / l_scr[...]


def fox_attention(z, col_q, col_k, col_v, c, ct, n_heads, seq_len, tq=512):
    n = z.shape[0]
    b = n // seq_len
    tk = tq
    nq = seq_len // tq
    hd = HEAD_DIM
    jq, jk, jv = col_q // hd, col_k // hd, col_v // hd
    return pl.pallas_call(
        functools.partial(_fox_kernel, tq=tq, tk=tk),
        out_shape=jax.ShapeDtypeStruct((n, n_heads * hd), F32),
        grid=(b, n_heads, nq, nq),
        in_specs=[
            pl.BlockSpec((tq, hd), lambda bi, hi, qi, ki: (bi * nq + qi, jq + hi)),
            pl.BlockSpec((tk, hd), lambda bi, hi, qi, ki: (bi * nq + jnp.minimum(ki, qi), jk + hi)),
            pl.BlockSpec((tk, hd), lambda bi, hi, qi, ki: (bi * nq + jnp.minimum(ki, qi), jv + hi)),
            pl.BlockSpec((tq, LANES), lambda bi, hi, qi, ki: (bi * nq + qi, 0)),
            pl.BlockSpec((None, SUBLANES, tk), lambda bi, hi, qi, ki: (bi, 0, jnp.minimum(ki, qi))),
        ],
        out_specs=pl.BlockSpec((tq, hd), lambda bi, hi, qi, ki: (bi * nq + qi, hi)),
        scratch_shapes=[pltpu.VMEM((tq, LANES), F32), pltpu.VMEM((tq, LANES), F32),
                        pltpu.VMEM((tq, hd), F32), pltpu.VMEM((tq, LANES), F32)],
        compiler_params=_cparams(("arbitrary", "arbitrary", "arbitrary", "arbitrary")),
        name="fox_attention",
    )(z, z, z, c, ct)


def _fox_paged_kernel(pt_ref, *refs, n_group, n_heads, t_pad, n_steps):
    del pt_ref
    g_ = n_group
    k_refs = refs[0:g_]
    v_refs = refs[g_:2 * g_]
    lf_refs = refs[2 * g_:3 * g_]
    (q_ref, kn_ref, vn_ref, cn_ref, ctn_ref, tri_ref, ones_ref, o_ref,
     m_scr, l_scr, acc_scr, carry_scr) = refs[3 * g_:]
    g = pl.program_id(1)
    scale = HEAD_DIM ** -0.5

    @pl.when(g == 0)
    def _():
        m_scr[...] = jnp.full_like(m_scr, NEG_INF)
        l_scr[...] = jnp.zeros_like(l_scr)
        acc_scr[...] = jnp.zeros_like(acc_scr)
        carry_scr[...] = jnp.zeros_like(carry_scr)

    lf_all = jnp.concatenate([r[...] for r in lf_refs], axis=0)
    hi, mid, lo = _split3(lf_all)
    tri = tri_ref[...]
    ones = ones_ref[...]
    after = _dot(hi, tri) + _dot(mid, tri) + _dot(lo, tri)
    total = _dot(hi, ones) + _dot(mid, ones) + _dot(lo, ones)
    carry = carry_scr[...]
    page_bias = []
    for i in range(g_):
        page_bias.append(carry + after[i * SUBLANES:(i + 1) * SUBLANES, :])
        carry = carry + total[i * SUBLANES:(i + 1) * SUBLANES, :]
    carry_scr[...] = carry

    cn = cn_ref[...]
    lane = lax.broadcasted_iota(jnp.int32, cn.shape, 1)

    def flash_update(h, s, v_list):
        m_prev = m_scr[h]
        m_new = jnp.maximum(m_prev, jnp.max(s, axis=1, keepdims=True))
        alpha = jnp.exp(m_prev - m_new)
        p32 = jnp.exp(s - m_new[:, 0:1])
        l_scr[h] = alpha * l_scr[h] + jnp.sum(p32, axis=1, keepdims=True)
        p = p32.astype(BF16)
        pv = _dot(p[:, 0:LANES], v_list[0])
        for i in range(1, len(v_list)):
            pv = pv + _dot(p[:, i * LANES:(i + 1) * LANES], v_list[i])
        acc_scr[h] = alpha * acc_scr[h] + pv
        m_scr[h] = m_new

    for h in range(n_heads):
        qh = q_ref[:, h * HEAD_DIM:(h + 1) * HEAD_DIM].astype(BF16)
        cq = jnp.sum(jnp.where(lane == h, cn, 0.0), axis=1, keepdims=True)
        s_list = []
        v_list = []
        for i in range(g_):
            kh = k_refs[i][:, h, :].astype(BF16)
            s_list.append(_dot_nt(qh, kh) * scale + (cq + page_bias[i][h:h + 1, :]))
            v_list.append(v_refs[i][:, h, :].astype(BF16))
        flash_update(h, jnp.concatenate(s_list, axis=1), v_list)

    @pl.when(g == n_steps - 1)
    def _():
        row = lax.broadcasted_iota(jnp.int32, (t_pad, LANES), 0)
        col = lax.broadcasted_iota(jnp.int32, (t_pad, LANES), 1)
        for h in range(n_heads):
            hs = slice(h * HEAD_DIM, (h + 1) * HEAD_DIM)
            qh = q_ref[:, hs].astype(BF16)
            cq = jnp.sum(jnp.where(lane == h, cn, 0.0), axis=1, keepdims=True)
            s = _dot_nt(qh, kn_ref[:, hs].astype(BF16)) * scale + (cq - ctn_ref[h:h + 1, :])
            s = jnp.where(col <= row, s, NEG_INF)
            flash_update(h, s, [vn_ref[:, hs].astype(BF16)])
            o_ref[:, hs] = acc_scr[h] / l_scr[h]


def fox_paged_attention(q, k_new, v_new, c_new, ct_new, cache_k, cache_v, logf_t, page_table, layer,
                        tri_after, ones_mat, n_group=8, t_pad=SUBLANES):
    b, n_pages = page_table.shape
    n_heads = cache_k.shape[3]
    page = cache_k.shape[2]
    hd = HEAD_DIM
    w = n_heads * hd
    ng = n_pages // n_group

    def page_of(i):
        return lambda bi, gi, pt: (pt[bi, n_pages - 1 - (gi * n_group + i)], layer, 0, 0, 0)

    def lf_of(i):
        return lambda bi, gi, pt: (layer, pt[bi, n_pages - 1 - (gi * n_group + i)], 0, 0)

    kv_specs = [pl.BlockSpec((None, None, page, n_heads, hd), page_of(i)) for i in range(n_group)]
    lf_specs = [pl.BlockSpec((None, None, SUBLANES, page), lf_of(i)) for i in range(n_group)]
    grid_spec = pltpu.PrefetchScalarGridSpec(
        num_scalar_prefetch=1,
        grid=(b, ng),
        in_specs=kv_specs + kv_specs + lf_specs + [
            pl.BlockSpec((None, t_pad, w), lambda bi, gi, pt: (bi, 0, 0)),
            pl.BlockSpec((None, LANES, w), lambda bi, gi, pt: (bi, 0, 0)),
            pl.BlockSpec((None, LANES, w), lambda bi, gi, pt: (bi, 0, 0)),
            pl.BlockSpec((t_pad, LANES), lambda bi, gi, pt: (bi * (LANES // t_pad), 0)),
            pl.BlockSpec((None, SUBLANES, LANES), lambda bi, gi, pt: (bi, 0, 0)),
            pl.BlockSpec((LANES, LANES), lambda bi, gi, pt: (0, 0)),
            pl.BlockSpec((LANES, LANES), lambda bi, gi, pt: (0, 0)),
        ],
        out_specs=pl.BlockSpec((None, t_pad, w), lambda bi, gi, pt: (bi, 0, 0)),
        scratch_shapes=[pltpu.VMEM((n_heads, t_pad, LANES), F32), pltpu.VMEM((n_heads, t_pad, LANES), F32),
                        pltpu.VMEM((n_heads, t_pad, hd), F32), pltpu.VMEM((SUBLANES, LANES), F32)],
    )
    return pl.pallas_call(
        functools.partial(_fox_paged_kernel, n_group=n_group, n_heads=n_heads, t_pad=t_pad, n_steps=ng),
        out_shape=jax.ShapeDtypeStruct((b, t_pad, w), F32),
        grid_spec=grid_spec,
        compiler_params=_cparams(("arbitrary", "arbitrary")),
        name="fox_paged_attention",
    )(page_table, *([cache_k] * n_group), *([cache_v] * n_group), *([logf_t] * n_group),
      q, k_new, v_new, c_new, ct_new, tri_after, ones_mat)


def _router_kernel(x_ref, sc_ref, sh_ref, wr_ref, br_ref, h_ref, info_ref, *, n_experts):
    h = x_ref[...] * (1.0 + sc_ref[...]) + sh_ref[...]
    h_ref[...] = h
    logits = _dot(h.astype(BF16), wr_ref[...].astype(BF16)) + br_ref[...]
    lane = lax.broadcasted_iota(jnp.int32, logits.shape, 1)
    neg = -jnp.inf
    l1 = jnp.where(lane < n_experts, logits, neg)
    m1 = jnp.max(l1, axis=1, keepdims=True)
    i1 = jnp.min(jnp.where(l1 == m1, lane, LANES), axis=1, keepdims=True)
    l2 = jnp.where(lane == i1, neg, l1)
    m2 = jnp.max(l2, axis=1, keepdims=True)
    i2 = jnp.min(jnp.where(l2 == m2, lane, LANES), axis=1, keepdims=True)
    e = jnp.exp(m2 - m1)
    den = 1.0 + e
    p1 = 1.0 / den
    p2 = e / den
    info = jnp.where(lane == 0, i1.astype(F32),
                     jnp.where(lane == 1, i2.astype(F32),
                               jnp.where(lane == 2, p1, jnp.where(lane == 3, p2, 0.0))))
    info_ref[...] = info


def moe_route(x, sc, sh, w_router_pad, b_router_pad, n_experts, tm, rows_per_seq):
    n, d = x.shape
    return pl.pallas_call(
        functools.partial(_router_kernel, n_experts=n_experts),
        out_shape=(jax.ShapeDtypeStruct((n, d), F32), jax.ShapeDtypeStruct((n, LANES), F32)),
        grid=(n // tm,),
        in_specs=[
            pl.BlockSpec((tm, d), lambda i: (i, 0)),
            _mod_spec(sc, tm, rows_per_seq, 1),
            _mod_spec(sh, tm, rows_per_seq, 1),
            pl.BlockSpec((d, LANES), lambda i: (0, 0)),
            pl.BlockSpec((1, LANES), lambda i: (0, 0)),
        ],
        out_specs=(pl.BlockSpec((tm, d), lambda i: (i, 0)), pl.BlockSpec((tm, LANES), lambda i: (i, 0))),
        compiler_params=_cparams(("arbitrary",)),
        name="moe_router",
    )(x, sc, sh, w_router_pad, b_router_pad)


def moe_plan(experts, n_experts, tm):
    n = experts.shape[0]
    na = n * TOP_K
    n_tiles = pl.cdiv(na, tm) + n_experts
    flat_e = experts.reshape(na)
    onehot = (flat_e[:, None] == jnp.arange(n_experts, dtype=jnp.int32)[None, :]).astype(jnp.int32)
    incl = jnp.cumsum(onehot, axis=0)
    rank = jnp.sum((incl - onehot) * onehot, axis=1)
    counts = incl[-1]
    tiles_e = (counts + tm - 1) // tm
    tile_end = jnp.cumsum(tiles_e)
    tile_off = tile_end - tiles_e
    n_used = tile_end[-1:]
    dest = (jnp.sum(tile_off[None, :] * onehot, axis=1) * tm + rank).astype(jnp.int32)
    src_token = jnp.zeros((n_tiles * tm,), jnp.int32).at[dest].set(jnp.arange(na, dtype=jnp.int32) // TOP_K)
    t = jnp.arange(n_tiles, dtype=jnp.int32)
    tile_expert = jnp.sum((t[:, None] >= tile_end[None, :]).astype(jnp.int32), axis=1)
    tile_expert = jnp.minimum(tile_expert, n_experts - 1).astype(jnp.int32)
    return dest, src_token, tile_expert, n_used.astype(jnp.int32)


def _row_copy(src_ref, row, dst_ref, slot, sem):
    return pltpu.make_async_copy(src_ref.at[pl.ds(row, 1), :], dst_ref.at[pl.ds(slot, 1), :], sem)


def _moe_gather_kernel(src_ref, nused_ref, hp_ref, hs_ref, o_ref, buf, sem, *, tm, n_first):
    t = pl.program_id(0)

    @pl.when(t < nused_ref[0])
    def _():
        base = t * tm

        def issue(r, carry):
            tok = src_ref[base + r]

            @pl.when(tok < n_first)
            def _():
                _row_copy(hp_ref, tok, buf, r, sem).start()

            @pl.when(tok >= n_first)
            def _():
                _row_copy(hs_ref, tok - n_first, buf, r, sem).start()

            return carry

        lax.fori_loop(0, tm, issue, 0)

        def drain(r, carry):
            _row_copy(hp_ref, 0, buf, r, sem).wait()
            return carry

        lax.fori_loop(0, tm, drain, 0)
        o_ref[...] = buf[...].astype(BF16)

    @pl.when(t >= nused_ref[0])
    def _():
        o_ref[...] = jnp.zeros_like(o_ref)


def moe_gather(h_first, h_second, src_token, n_used, tm):
    s = src_token.shape[0]
    d = h_first.shape[1]
    n_tiles = s // tm
    grid_spec = pltpu.PrefetchScalarGridSpec(
        num_scalar_prefetch=2,
        grid=(n_tiles,),
        in_specs=[pl.BlockSpec(memory_space=pl.ANY), pl.BlockSpec(memory_space=pl.ANY)],
        out_specs=pl.BlockSpec((tm, d), lambda t, src, nu: (t, 0)),
        scratch_shapes=[pltpu.VMEM((tm, d), F32), pltpu.SemaphoreType.DMA(())],
    )
    return pl.pallas_call(
        functools.partial(_moe_gather_kernel, tm=tm, n_first=h_first.shape[0]),
        out_shape=jax.ShapeDtypeStruct((s, d), BF16),
        grid_spec=grid_spec,
        compiler_params=_cparams(("arbitrary",)),
        name="moe_gather",
    )(src_token, n_used, h_first, h_second)


def _moe_up_kernel(te_ref, nused_ref, x_ref, wg_ref, wu_ref, a_ref):
    del te_ref

    @pl.when(pl.program_id(1) < nused_ref[0])
    def _():
        xb = x_ref[...]
        gate = _dot(xb, wg_ref[...].astype(BF16))
        up = _dot(xb, wu_ref[...].astype(BF16))
        a_ref[...] = (_silu(gate) * up).astype(a_ref.dtype)

    @pl.when(pl.program_id(1) >= nused_ref[0])
    def _():
        a_ref[...] = jnp.zeros_like(a_ref)


def moe_up(xs, w_gate, w_up, idx, tile_expert, n_used, tm, tf=512):
    s, d = xs.shape
    ff = w_gate.shape[3]
    n_tiles = s // tm

    def tile(t, nu):
        return jnp.minimum(t, nu[0] - 1)

    grid_spec = pltpu.PrefetchScalarGridSpec(
        num_scalar_prefetch=2,
        grid=(pl.cdiv(ff, tf), n_tiles),
        in_specs=[
            pl.BlockSpec((tm, d), lambda f, t, te, nu: (tile(t, nu), 0)),
            pl.BlockSpec((None, None, d, tf), lambda f, t, te, nu: (idx, te[tile(t, nu)], 0, f)),
            pl.BlockSpec((None, None, d, tf), lambda f, t, te, nu: (idx, te[tile(t, nu)], 0, f)),
        ],
        out_specs=pl.BlockSpec((tm, tf), lambda f, t, te, nu: (t, f)),
    )
    return pl.pallas_call(
        _moe_up_kernel,
        out_shape=jax.ShapeDtypeStruct((s, ff), BF16),
        grid_spec=grid_spec,
        compiler_params=_cparams(("arbitrary", "arbitrary")),
        name="moe_up",
    )(tile_expert, n_used, xs, w_gate, w_up)


def _moe_down_kernel(te_ref, nused_ref, a_ref, w_ref, o_ref):
    del te_ref

    @pl.when(pl.program_id(1) < nused_ref[0])
    def _():
        o_ref[...] = _dot(a_ref[...], w_ref[...].astype(BF16))

    @pl.when(pl.program_id(1) >= nused_ref[0])
    def _():
        o_ref[...] = jnp.zeros_like(o_ref)


def moe_down(a, w_down, idx, tile_expert, n_used, tm, tn=512):
    s, ff = a.shape
    d = w_down.shape[3]
    n_tiles = s // tm

    def tile(t, nu):
        return jnp.minimum(t, nu[0] - 1)

    grid_spec = pltpu.PrefetchScalarGridSpec(
        num_scalar_prefetch=2,
        grid=(d // tn, n_tiles),
        in_specs=[
            pl.BlockSpec((tm, ff), lambda j, t, te, nu: (tile(t, nu), 0)),
            pl.BlockSpec((None, None, ff, tn), lambda j, t, te, nu: (idx, te[tile(t, nu)], 0, j)),
        ],
        out_specs=pl.BlockSpec((tm, tn), lambda j, t, te, nu: (t, j)),
    )
    return pl.pallas_call(
        _moe_down_kernel,
        out_shape=jax.ShapeDtypeStruct((s, d), F32),
        grid_spec=grid_spec,
        compiler_params=_cparams(("arbitrary", "arbitrary")),
        name="moe_down",
    )(tile_expert, n_used, a, w_down)


def _moe_combine_kernel(dest_ref, y_ref, info_ref, x_ref, g_ref, w_ref, b_ref, o_ref, buf0, buf1, sem,
                        *, tm, row0, alpha):
    base = (row0 + pl.program_id(0) * tm) * TOP_K

    def issue(r, carry):
        _row_copy(y_ref, dest_ref[base + TOP_K * r], buf0, r, sem).start()
        _row_copy(y_ref, dest_ref[base + TOP_K * r + 1], buf1, r, sem).start()
        return carry

    lax.fori_loop(0, tm, issue, 0)

    def drain(r, carry):
        _row_copy(y_ref, 0, buf0, r, sem).wait()
        _row_copy(y_ref, 0, buf1, r, sem).wait()
        return carry

    lax.fori_loop(0, tm, drain, 0)
    info = info_ref[...]
    f = info[:, 2:3] * buf0[...] + info[:, 3:4] * buf1[...]
    y = alpha * x_ref[...] + g_ref[...] * f
    o_ref[...] = _layer_norm_rows(y, w_ref[...], b_ref[...])


def moe_combine_ln(y_sorted, dest, info, x, g, w, b, row0, alpha, tm, rows_per_seq):
    n, d = x.shape
    gspec = _mod_spec(g, tm, rows_per_seq, 1)
    gmap = gspec.index_map
    grid_spec = pltpu.PrefetchScalarGridSpec(
        num_scalar_prefetch=1,
        grid=(n // tm,),
        in_specs=[
            pl.BlockSpec(memory_space=pl.ANY),
            pl.BlockSpec((tm, LANES), lambda i, dst: (i, 0)),
            pl.BlockSpec((tm, d), lambda i, dst: (i, 0)),
            pl.BlockSpec(gspec.block_shape, lambda i, dst: gmap(i)),
            pl.BlockSpec((1, d), lambda i, dst: (0, 0)),
            pl.BlockSpec((1, d), lambda i, dst: (0, 0)),
        ],
        out_specs=pl.BlockSpec((tm, d), lambda i, dst: (i, 0)),
        scratch_shapes=[pltpu.VMEM((tm, d), F32), pltpu.VMEM((tm, d), F32), pltpu.SemaphoreType.DMA(())],
    )
    return pl.pallas_call(
        functools.partial(_moe_combine_kernel, tm=tm, row0=row0, alpha=alpha),
        out_shape=jax.ShapeDtypeStruct((n, d), F32),
        grid_spec=grid_spec,
        compiler_params=_cparams(("arbitrary",)),
        name="moe_combine_ln",
    )(dest, y_sorted, info, x, g, w.reshape(1, d), b.reshape(1, d))


PROMPT_TM = 1024
PROMPT_LN_TM = 512
MOE_TM = 512
MOE_COMBINE_TM = 256


def _pad_axis(a, axis, size):
    pad = [(0, 0)] * a.ndim
    pad[axis] = (0, size - a.shape[axis])
    return jnp.pad(a, pad)


def kernel(x_prompt, x_sample, cache_k, cache_v, cache_logf, state_ret, state_conv, page_table, c_prompt, c_sample, w_ada, b_ada, w_in, b_forget, ret_gn_w, conv_w, conv_b, conv_ln_w, conv_ln_b, w_out, ln1_w, ln1_b, ln2_w, ln2_b, ffn_w_gate, ffn_w_up, ffn_w_down, moe_router, moe_router_b, moe_w_gate, moe_w_up, moe_w_down):
    bp, tp, d = x_prompt.shape
    bs, ts, _ = x_sample.shape
    depth = w_in.shape[0]
    ret_heads = state_ret.shape[2]
    fox_heads = cache_k.shape[3]
    page = cache_k.shape[2]
    conv_dim = state_conv.shape[3]
    n_experts = moe_router.shape[2]
    hd = HEAD_DIM
    ret_w, fox_w = ret_heads * hd, fox_heads * hd
    o_ca = 4 * ret_w
    o_cb = o_ca + conv_dim
    o_fq = o_cb + conv_dim
    o_fk = o_fq + fox_w
    o_fv = o_fk + fox_w
    n_main = o_fv + fox_w
    alpha = (2 * depth) ** 0.25
    past_len = page_table.shape[1] * page
    tpad = RET_CHUNK
    np_rows, ns_rows = bp * tp, bs * ts

    w_forget = _pad_axis(w_in[:, :, n_main:], 2, LANES)
    bf_rows = _pad_axis(b_forget, 1, LANES).reshape(depth, 1, LANES)
    router_w = _pad_axis(moe_router, 2, LANES)
    router_b = _pad_axis(moe_router_b, 1, LANES).reshape(-1, 1, LANES)
    conv_w_pad = _pad_axis(conv_w, 1, CONV_HALO)
    logf_t = _pad_axis(jnp.transpose(cache_logf, (1, 0, 3, 2)), 2, SUBLANES)
    tri_incl = jnp.tril(jnp.ones((LANES, LANES), BF16))
    tri_after = jnp.tril(jnp.ones((page, page), BF16), -1)
    ones_mat = jnp.ones((page, page), BF16)
    tabs_p = retention_tables(ret_heads, tp, 0, tp)
    tabs_s = retention_tables(ret_heads, ts, past_len, tpad)
    r0_p = jnp.zeros((bp, ret_heads, hd, hd), F32)
    buf0_p = jnp.zeros((bp, CONV_HALO, conv_dim), F32)

    c_rows = _pad_axis(jnp.concatenate([c_prompt, c_sample], axis=0), 0, 2 * SUBLANES)
    mod = ada_modulation(c_rows, w_ada, b_ada)

    xp = x_prompt.reshape(np_rows, d)
    xs = x_sample.reshape(ns_rows, d)
    rets_p, rets_s, convs_p, convs_s, ks_p, ks_s, vs_p, vs_s, lfs_p, lfs_s = ([] for _ in range(10))

    def unpad(a):
        return a.reshape(bs, tpad, -1)[:, :ts].reshape(ns_rows, -1)

    for l in range(depth):
        parts = [mod[l, :, i * d:(i + 1) * d] for i in range(6)]
        sh1p, sc1p, g1p, sh2p, sc2p, g2p = (p[:bp].reshape(bp, 1, d) for p in parts)
        sh1s, sc1s, g1s, sh2s, sc2s, g2s = (jnp.repeat(p[bp:bp + bs], ts, axis=0) for p in parts)

        z, zf = proj_in(xp, sc1p, sh1p, w_in, w_forget, l, n_main, PROMPT_TM, tp, tn=640)
        ret_o, r_new = retention(z, tabs_p, ret_gn_w[l], r0_p, ret_heads, tp)
        conv_o, conv_new = conv_branch(z, o_ca, o_cb, buf0_p, conv_w_pad[l], conv_b[l], conv_ln_w[l],
                                       conv_ln_b[l], tp, RET_CHUNK)
        lf, csum, csum_t = forget_cumsum(zf, bf_rows[l], tri_incl, tp)
        fox_o = fox_attention(z, o_fq, o_fk, o_fv, csum, csum_t, fox_heads, tp)
        m = proj_out(ret_o, conv_o, fox_o, w_out, l, PROMPT_TM)
        x1p = ln_residual(xp, m, g1p, ln1_w[l], ln1_b[l], alpha, PROMPT_LN_TM, tp)
        rets_p.append(r_new)
        convs_p.append(conv_new[:, CONV_HALO - (CONV_W - 1):])
        ks_p.append(z[:, o_fk:o_fv].reshape(bp, tp, fox_heads, hd))
        vs_p.append(z[:, o_fv:n_main].reshape(bp, tp, fox_heads, hd))
        lfs_p.append(lf[:, :fox_heads].reshape(bp, tp, fox_heads))

        zs, zfs = proj_in(xs, sc1s, sh1s, w_in, w_forget, l, n_main, ns_rows, ts, tn=640)
        zs_pad = _pad_axis(zs.reshape(bs, ts, n_main), 1, tpad)
        z2 = zs_pad.reshape(bs * tpad, n_main)
        zf2 = _pad_axis(zfs.reshape(bs, ts, LANES), 1, tpad).reshape(bs * tpad, LANES)
        ret_s, r_new_s = retention(z2, tabs_s, ret_gn_w[l], state_ret[:, l], ret_heads, tpad)
        buf_s = jnp.pad(state_conv[:, l], ((0, 0), (CONV_HALO - (CONV_W - 1), 0), (0, 0)))
        conv_s, conv_new_s = conv_branch(z2, o_ca, o_cb, buf_s, conv_w_pad[l], conv_b[l], conv_ln_w[l],
                                         conv_ln_b[l], tpad, ts)
        lf_s, csum_s, csum_t_s = forget_cumsum(zf2, bf_rows[l], tri_incl, tpad)
        fox_s = fox_paged_attention(zs_pad[:, :SUBLANES, o_fq:o_fk], zs_pad[:, :, o_fk:o_fv],
                                    zs_pad[:, :, o_fv:n_main], csum_s, csum_t_s, cache_k, cache_v, logf_t,
                                    page_table, l, tri_after, ones_mat)
        fox_s = fox_s[:, :ts].reshape(ns_rows, fox_w)
        m_s = proj_out(unpad(ret_s), unpad(conv_s), fox_s, w_out, l, ns_rows)
        x1s = ln_residual(xs, m_s, g1s, ln1_w[l], ln1_b[l], alpha, ns_rows, ts)
        rets_s.append(r_new_s)
        convs_s.append(conv_new_s[:, CONV_HALO - (CONV_W - 1):])
        ks_s.append(zs[:, o_fk:o_fv].reshape(bs, ts, fox_heads, hd))
        vs_s.append(zs[:, o_fv:n_main].reshape(bs, ts, fox_heads, hd))
        lfs_s.append(unpad(lf_s)[:, :fox_heads].reshape(bs, ts, fox_heads))

        i = l // 2
        if l % 2 == 0:
            a = ffn_up(x1p, sc2p, sh2p, ffn_w_gate, ffn_w_up, i, PROMPT_TM, tp)
            y = ffn_down(a, ffn_w_down, i, PROMPT_TM)
            xp = ln_residual(x1p, y, g2p, ln2_w[l], ln2_b[l], alpha, PROMPT_LN_TM, tp)
            a_s = ffn_up(x1s, sc2s, sh2s, ffn_w_gate, ffn_w_up, i, ns_rows, ts)
            y_s = ffn_down(a_s, ffn_w_down, i, ns_rows)
            xs = ln_residual(x1s, y_s, g2s, ln2_w[l], ln2_b[l], alpha, ns_rows, ts)
        else:
            h_p, info_p = moe_route(x1p, sc2p, sh2p, router_w[i], router_b[i], n_experts, PROMPT_LN_TM, tp)
            h_s, info_s = moe_route(x1s, sc2s, sh2s, router_w[i], router_b[i], n_experts, ns_rows, ts)
            experts = jnp.concatenate([info_p[:, :TOP_K], info_s[:, :TOP_K]], axis=0).astype(jnp.int32)
            dest, src_token, tile_expert, n_used = moe_plan(experts, n_experts, MOE_TM)
            x_sorted = moe_gather(h_p, h_s, src_token, n_used, MOE_TM)
            a = moe_up(x_sorted, moe_w_gate, moe_w_up, i, tile_expert, n_used, MOE_TM)
            y = moe_down(a, moe_w_down, i, tile_expert, n_used, MOE_TM)
            xp = moe_combine_ln(y, dest, info_p, x1p, g2p, ln2_w[l], ln2_b[l], 0, alpha, MOE_COMBINE_TM, tp)
            xs = moe_combine_ln(y, dest, info_s, x1s, g2s, ln2_w[l], ln2_b[l], np_rows, alpha, ns_rows, ts)

    st = lambda lst: jnp.stack(lst, axis=1)
    return (xp.reshape(bp, tp, d), xs.reshape(bs, ts, d), st(rets_p), st(rets_s), st(convs_p), st(convs_s),
            st(ks_p), st(ks_s), st(vs_p), st(vs_s), st(lfs_p), st(lfs_s))
```

```python
import functools

import jax
import jax.numpy as jnp
from jax import lax
from jax.experimental import pallas as pl
from jax.experimental.pallas import tpu as pltpu

F32 = jnp.float32
BF16 = jnp.bfloat16

LANES = 128
SUBLANES = 8
VMEM_LIMIT_BYTES = 56 * 1024 * 1024

HEAD_DIM = 128
CONV_W = 31
CONV_HALO = 32
RET_CHUNK = 128
TOP_K = 2
LN_EPS = 1e-5
NEG_INF = -1e30
ROPE_BASE = 10000.0


def _cparams(sem, vmem=VMEM_LIMIT_BYTES):
    return pltpu.CompilerParams(dimension_semantics=sem, vmem_limit_bytes=vmem)


def _silu(x):
    return x * jax.nn.sigmoid(x)


def _dot(a, b):
    return jnp.dot(a, b, preferred_element_type=F32)


def _dot_nt(a, b):
    return lax.dot_general(a, b, (((1,), (1,)), ((), ())), preferred_element_type=F32)


def _split3(x):
    hi = x.astype(BF16)
    r1 = x - hi.astype(F32)
    mid = r1.astype(BF16)
    lo = (r1 - mid.astype(F32)).astype(BF16)
    return hi, mid, lo


def _log_sigmoid(x):
    return -(jnp.maximum(-x, 0.0) + jnp.log1p(jnp.exp(-jnp.abs(x))))


def _layer_norm_rows(y, w, b):
    mu = jnp.mean(y, axis=-1, keepdims=True)
    d = y - mu
    var = jnp.mean(d * d, axis=-1, keepdims=True)
    return d * lax.rsqrt(var + LN_EPS) * w + b


def _ada_kernel(c_ref, w_ref, b_ref, o_ref):
    cs = _silu(c_ref[...]).astype(BF16)
    o_ref[...] = _dot(cs, w_ref[...].astype(BF16)) + b_ref[...]


def ada_modulation(c_rows, w_ada, b_ada, tn=1024):
    depth, d, n6 = w_ada.shape
    r = c_rows.shape[0]
    return pl.pallas_call(
        _ada_kernel,
        out_shape=jax.ShapeDtypeStruct((depth, r, n6), F32),
        grid=(depth, n6 // tn),
        in_specs=[
            pl.BlockSpec((r, d), lambda l, j: (0, 0)),
            pl.BlockSpec((None, d, tn), lambda l, j: (l, 0, j)),
            pl.BlockSpec((None, 1, tn), lambda l, j: (l, 0, j)),
        ],
        out_specs=pl.BlockSpec((None, r, tn), lambda l, j: (l, 0, j)),
        compiler_params=_cparams(("arbitrary", "arbitrary")),
        name="ada_modulation",
    )(c_rows, w_ada, b_ada.reshape(depth, 1, n6))


def _mod_spec(arr, tm, rows_per_seq, grid_rank):
    d = arr.shape[-1]
    if arr.ndim == 3:
        assert rows_per_seq % tm == 0
        per = rows_per_seq // tm
        if grid_rank == 1:
            return pl.BlockSpec((None, 1, d), lambda i: (i // per, 0, 0))
        return pl.BlockSpec((None, 1, d), lambda i, j: (i // per, 0, 0))
    if grid_rank == 1:
        return pl.BlockSpec((tm, d), lambda i: (i, 0))
    return pl.BlockSpec((tm, d), lambda i, j: (i, 0))


def _proj_in_kernel(x_ref, sc_ref, sh_ref, w_ref, wf_ref, z_ref, zf_ref, xb_ref):
    @pl.when(pl.program_id(1) == 0)
    def _():
        h = x_ref[...] * (1.0 + sc_ref[...]) + sh_ref[...]
        xb_ref[...] = h.astype(BF16)
        zf_ref[...] = _dot(xb_ref[...], wf_ref[...].astype(BF16))

    z_ref[...] = _dot(xb_ref[...], w_ref[...].astype(BF16))


def proj_in(x, sc, sh, w_in, w_forget, layer, n_main, tm, rows_per_seq, tn=512):
    n, d = x.shape
    return pl.pallas_call(
        _proj_in_kernel,
        out_shape=(jax.ShapeDtypeStruct((n, n_main), F32), jax.ShapeDtypeStruct((n, LANES), F32)),
        grid=(n // tm, n_main // tn),
        in_specs=[
            pl.BlockSpec((tm, d), lambda i, j: (i, 0)),
            _mod_spec(sc, tm, rows_per_seq, 2),
            _mod_spec(sh, tm, rows_per_seq, 2),
            pl.BlockSpec((None, d, tn), lambda i, j: (layer, 0, j)),
            pl.BlockSpec((None, d, LANES), lambda i, j: (layer, 0, 0)),
        ],
        out_specs=(pl.BlockSpec((tm, tn), lambda i, j: (i, j)),
                   pl.BlockSpec((tm, LANES), lambda i, j: (i, 0))),
        scratch_shapes=[pltpu.VMEM((tm, d), BF16)],
        compiler_params=_cparams(("arbitrary", "arbitrary")),
        name="proj_in",
    )(x, sc, sh, w_in, w_forget)


def _proj_out_kernel(a_ref, b_ref, c_ref, w_ref, o_ref, xb_ref):
    @pl.when(pl.program_id(1) == 0)
    def _():
        wa, wb = a_ref.shape[1], b_ref.shape[1]
        xb_ref[:, 0:wa] = a_ref[...].astype(BF16)
        xb_ref[:, wa:wa + wb] = b_ref[...].astype(BF16)
        xb_ref[:, wa + wb:] = c_ref[...].astype(BF16)

    o_ref[...] = _dot(xb_ref[...], w_ref[...].astype(BF16))


def proj_out(a, b, c, w_out, layer, tm, tn=512):
    n = a.shape[0]
    d = w_out.shape[1]
    dout = w_out.shape[2]
    return pl.pallas_call(
        _proj_out_kernel,
        out_shape=jax.ShapeDtypeStruct((n, dout), F32),
        grid=(n // tm, dout // tn),
        in_specs=[
            pl.BlockSpec((tm, a.shape[1]), lambda i, j: (i, 0)),
            pl.BlockSpec((tm, b.shape[1]), lambda i, j: (i, 0)),
            pl.BlockSpec((tm, c.shape[1]), lambda i, j: (i, 0)),
            pl.BlockSpec((None, d, tn), lambda i, j: (layer, 0, j)),
        ],
        out_specs=pl.BlockSpec((tm, tn), lambda i, j: (i, j)),
        scratch_shapes=[pltpu.VMEM((tm, d), BF16)],
        compiler_params=_cparams(("arbitrary", "arbitrary")),
        name="proj_out",
    )(a, b, c, w_out)


def _ln_res_kernel(x_ref, m_ref, g_ref, w_ref, b_ref, o_ref, *, alpha):
    y = alpha * x_ref[...] + g_ref[...] * m_ref[...]
    o_ref[...] = _layer_norm_rows(y, w_ref[...], b_ref[...])


def ln_residual(x, m, g, w, b, alpha, tm, rows_per_seq):
    n, d = x.shape
    return pl.pallas_call(
        functools.partial(_ln_res_kernel, alpha=alpha),
        out_shape=jax.ShapeDtypeStruct((n, d), F32),
        grid=(n // tm,),
        in_specs=[
            pl.BlockSpec((tm, d), lambda i: (i, 0)),
            pl.BlockSpec((tm, d), lambda i: (i, 0)),
            _mod_spec(g, tm, rows_per_seq, 1),
            pl.BlockSpec((1, d), lambda i: (0, 0)),
            pl.BlockSpec((1, d), lambda i: (0, 0)),
        ],
        out_specs=pl.BlockSpec((tm, d), lambda i: (i, 0)),
        compiler_params=_cparams(("arbitrary",)),
        name="ln_residual",
    )(x, m, g, w.reshape(1, d), b.reshape(1, d))


def _ffn_up_kernel(x_ref, sc_ref, sh_ref, wg_ref, wu_ref, a_ref, xb_ref):
    @pl.when(pl.program_id(1) == 0)
    def _():
        h = x_ref[...] * (1.0 + sc_ref[...]) + sh_ref[...]
        xb_ref[...] = h.astype(BF16)

    xb = xb_ref[...]
    gate = _dot(xb, wg_ref[...].astype(BF16))
    up = _dot(xb, wu_ref[...].astype(BF16))
    a_ref[...] = (_silu(gate) * up).astype(a_ref.dtype)


def ffn_up(x, sc, sh, w_gate, w_up, idx, tm, rows_per_seq, tf=512):
    n, d = x.shape
    ff = w_gate.shape[2]
    return pl.pallas_call(
        _ffn_up_kernel,
        out_shape=jax.ShapeDtypeStruct((n, ff), BF16),
        grid=(n // tm, pl.cdiv(ff, tf)),
        in_specs=[
            pl.BlockSpec((tm, d), lambda i, j: (i, 0)),
            _mod_spec(sc, tm, rows_per_seq, 2),
            _mod_spec(sh, tm, rows_per_seq, 2),
            pl.BlockSpec((None, d, tf), lambda i, j: (idx, 0, j)),
            pl.BlockSpec((None, d, tf), lambda i, j: (idx, 0, j)),
        ],
        out_specs=pl.BlockSpec((tm, tf), lambda i, j: (i, j)),
        scratch_shapes=[pltpu.VMEM((tm, d), BF16)],
        compiler_params=_cparams(("arbitrary", "arbitrary")),
        name="ffn_up",
    )(x, sc, sh, w_gate, w_up)


def _ffn_down_kernel(a_ref, w_ref, o_ref):
    o_ref[...] = _dot(a_ref[...], w_ref[...].astype(BF16))


def ffn_down(a, w_down, idx, tm, tn=256):
    n, ff = a.shape
    d = w_down.shape[2]
    return pl.pallas_call(
        _ffn_down_kernel,
        out_shape=jax.ShapeDtypeStruct((n, d), F32),
        grid=(n // tm, d // tn),
        in_specs=[
            pl.BlockSpec((tm, ff), lambda i, j: (i, 0)),
            pl.BlockSpec((None, ff, tn), lambda i, j: (idx, 0, j)),
        ],
        out_specs=pl.BlockSpec((tm, tn), lambda i, j: (i, j)),
        compiler_params=_cparams(("arbitrary", "arbitrary")),
        name="ffn_down",
    )(a, w_down)


def _retention_kernel(q_ref, k_ref, v_ref, g_ref, cos_ref, sin_ref, dmat_ref, xi_ref, zeta_ref,
                      gc_ref, gnw_ref, r0_ref, y_ref, rn_ref, r_scr, *, n_chunks, n_heads):
    c = pl.program_id(1)

    @pl.when(c == 0)
    def _():
        r_scr[...] = r0_ref[...]

    cos = cos_ref[...]
    sin = sin_ref[...]

    def rope(x):
        return x * cos + pltpu.roll(x, HEAD_DIM // 2, axis=1) * sin

    for h in range(n_heads):
        hs = slice(h * HEAD_DIM, (h + 1) * HEAD_DIM)
        q = rope(q_ref[:, hs])
        k = rope(k_ref[:, hs]) * (HEAD_DIM ** -0.5)
        qb = q.astype(BF16)
        kb = k.astype(BF16)
        vb = v_ref[:, hs].astype(BF16)
        r_prev = r_scr[h]

        s = _dot_nt(qb, kb) * dmat_ref[h]
        y = _dot(s.astype(BF16), vb) + _dot(qb, r_prev.astype(BF16)) * xi_ref[h]
        kz_t = jnp.transpose(k * zeta_ref[h]).astype(BF16)
        r_scr[h] = gc_ref[h] * r_prev + _dot(kz_t, vb)

        mu = jnp.mean(y, axis=-1, keepdims=True)
        d = y - mu
        var = jnp.mean(d * d, axis=-1, keepdims=True)
        yn = d * lax.rsqrt(var + LN_EPS) * gnw_ref[:, hs]
        y_ref[:, hs] = yn * _silu(g_ref[:, hs])

    @pl.when(c == n_chunks - 1)
    def _():
        rn_ref[...] = r_scr[...]


def retention(z, tabs, gn_w, r0, n_heads, seq_len):
    n = z.shape[0]
    b = n // seq_len
    cch = RET_CHUNK
    nc = seq_len // cch
    hd = HEAD_DIM
    h_ = n_heads
    w = h_ * hd
    const3 = lambda bi, ci: (0, 0, 0)
    return pl.pallas_call(
        functools.partial(_retention_kernel, n_chunks=nc, n_heads=h_),
        out_shape=(jax.ShapeDtypeStruct((n, w), F32), jax.ShapeDtypeStruct((b, h_, hd, hd), F32)),
        grid=(b, nc),
        in_specs=[
            pl.BlockSpec((cch, w), lambda bi, ci: (bi * nc + ci, 0)),
            pl.BlockSpec((cch, w), lambda bi, ci: (bi * nc + ci, 1)),
            pl.BlockSpec((cch, w), lambda bi, ci: (bi * nc + ci, 2)),
            pl.BlockSpec((cch, w), lambda bi, ci: (bi * nc + ci, 3)),
            pl.BlockSpec((cch, hd), lambda bi, ci: (ci, 0)),
            pl.BlockSpec((cch, hd), lambda bi, ci: (ci, 0)),
            pl.BlockSpec((h_, cch, cch), const3),
            pl.BlockSpec((h_, cch, hd), const3),
            pl.BlockSpec((h_, cch, hd), const3),
            pl.BlockSpec((h_, 1, hd), const3),
            pl.BlockSpec((1, w), lambda bi, ci: (0, 0)),
            pl.BlockSpec((None, h_, hd, hd), lambda bi, ci: (bi, 0, 0, 0)),
        ],
        out_specs=(pl.BlockSpec((cch, w), lambda bi, ci: (bi * nc + ci, 0)),
                   pl.BlockSpec((None, h_, hd, hd), lambda bi, ci: (bi, 0, 0, 0))),
        scratch_shapes=[pltpu.VMEM((h_, hd, hd), F32)],
        compiler_params=_cparams(("arbitrary", "arbitrary")),
        name="retention",
    )(z, z, z, z, tabs["cos"], tabs["sin"], tabs["dmat"], tabs["xi"], tabs["zeta"], tabs["gc"],
      gn_w.reshape(1, w), r0)


def retention_tables(n_heads, t_valid, pos0, t_pad):
    cch = min(t_valid, RET_CHUNK)
    half = HEAD_DIM // 2
    lg = jnp.log(1.0 - 2.0 ** (-5.0 - jnp.arange(n_heads, dtype=F32)))
    idx = jnp.arange(cch, dtype=F32)
    diff = idx[:, None] - idx[None, :]
    dmat = jnp.where(diff[None] >= 0, jnp.exp(jnp.maximum(diff, 0.0)[None] * lg[:, None, None]), 0.0)
    xi = jnp.exp((idx[None, :] + 1.0) * lg[:, None])
    zeta = jnp.exp((cch - 1.0 - idx)[None, :] * lg[:, None])
    gc = jnp.exp(cch * lg)
    pc = RET_CHUNK - cch
    dmat = jnp.pad(dmat, ((0, 0), (0, pc), (0, pc)))
    xi = jnp.broadcast_to(jnp.pad(xi, ((0, 0), (0, pc)))[:, :, None], (n_heads, RET_CHUNK, HEAD_DIM))
    zeta = jnp.broadcast_to(jnp.pad(zeta, ((0, 0), (0, pc)))[:, :, None], (n_heads, RET_CHUNK, HEAD_DIM))
    gc = jnp.broadcast_to(gc[:, None, None], (n_heads, 1, HEAD_DIM))
    pos = (pos0 + jnp.arange(t_pad)).astype(F32)
    inv = ROPE_BASE ** (-jnp.arange(half, dtype=F32) / half)
    ang = pos[:, None] * inv[None, :]
    cos = jnp.cos(ang)
    sin = jnp.sin(ang)
    return dict(dmat=dmat, xi=xi, zeta=zeta, gc=gc,
                cos=jnp.concatenate([cos, cos], axis=1), sin=jnp.concatenate([-sin, sin], axis=1))


CONV_ROWS = 32


def _conv_kernel(za_ref, zb_ref, buf_ref, cw_ref, cb_ref, lw_ref, lb_ref, o_ref, new_ref, u_scr,
                 *, tt, t_last, n_tiles):
    t = pl.program_id(1)

    @pl.when(t == 0)
    def _():
        u_scr[0:CONV_HALO, :] = buf_ref[...]

    @pl.when(t > 0)
    def _():
        u_scr[0:CONV_HALO, :] = u_scr[tt:tt + CONV_HALO, :]

    u_scr[CONV_HALO:CONV_HALO + tt, :] = za_ref[...] * jax.nn.sigmoid(zb_ref[...])

    off = CONV_HALO - (CONV_W - 1)
    for r0 in range(0, tt, CONV_ROWS):
        acc = u_scr[r0 + off:r0 + off + CONV_ROWS, :] * cw_ref[0:1, :]
        for w in range(1, CONV_W):
            acc = acc + u_scr[r0 + off + w:r0 + off + w + CONV_ROWS, :] * cw_ref[w:w + 1, :]
        y = _layer_norm_rows(acc + cb_ref[...], lw_ref[...], lb_ref[...])
        o_ref[r0:r0 + CONV_ROWS, :] = _silu(y)

    @pl.when(t == n_tiles - 1)
    def _():
        new_ref[...] = u_scr[t_last:t_last + CONV_HALO, :]


def conv_branch(z, col_a, col_b, conv_buf, cw, cb, lw, lb, seq_len, t_last, tt=128):
    n = z.shape[0]
    b = n // seq_len
    nt = seq_len // tt
    cdim = cw.shape[1]
    ja, jb = col_a // cdim, col_b // cdim
    return pl.pallas_call(
        functools.partial(_conv_kernel, tt=tt, t_last=t_last, n_tiles=nt),
        out_shape=(jax.ShapeDtypeStruct((n, cdim), F32), jax.ShapeDtypeStruct((b, CONV_HALO, cdim), F32)),
        grid=(b, nt),
        in_specs=[
            pl.BlockSpec((tt, cdim), lambda bi, ti: (bi * nt + ti, ja)),
            pl.BlockSpec((tt, cdim), lambda bi, ti: (bi * nt + ti, jb)),
            pl.BlockSpec((None, CONV_HALO, cdim), lambda bi, ti: (bi, 0, 0)),
            pl.BlockSpec((CONV_HALO, cdim), lambda bi, ti: (0, 0)),
            pl.BlockSpec((1, cdim), lambda bi, ti: (0, 0)),
            pl.BlockSpec((1, cdim), lambda bi, ti: (0, 0)),
            pl.BlockSpec((1, cdim), lambda bi, ti: (0, 0)),
        ],
        out_specs=(pl.BlockSpec((tt, cdim), lambda bi, ti: (bi * nt + ti, 0)),
                   pl.BlockSpec((None, CONV_HALO, cdim), lambda bi, ti: (bi, 0, 0))),
        scratch_shapes=[pltpu.VMEM((CONV_HALO + tt, cdim), F32)],
        compiler_params=_cparams(("arbitrary", "arbitrary")),
        name="conv_branch",
    )(z, z, conv_buf, cw, cb.reshape(1, cdim), lw.reshape(1, cdim), lb.reshape(1, cdim))


def _tri_cumsum(tri_ref, x):
    hi, mid, lo = _split3(x)
    tri = tri_ref[...]
    return _dot(tri, hi) + _dot(tri, mid) + _dot(tri, lo)


LOG2E = 1.4426950408889634
N_BIAS_TERMS = 3


def _logf_kernel(zf_ref, bf_ref, tri_ref, lf_ref, c_ref, ct_ref, fq_ref, fk_ref, carry_ref, *, n_heads):
    @pl.when(pl.program_id(1) == 0)
    def _():
        carry_ref[...] = jnp.zeros_like(carry_ref)

    lf = _log_sigmoid(zf_ref[...] + bf_ref[...])
    c = _tri_cumsum(tri_ref, lf) + carry_ref[...]
    carry_ref[...] = c[LANES - 1:LANES, :]
    lf_ref[...] = lf
    c_ref[...] = c
    ct_ref[...] = jnp.transpose(c)[0:SUBLANES, :]

    lane = lax.broadcasted_iota(jnp.int32, c.shape, 1)
    ones_lanes = jnp.where(lane < 2 * N_BIAS_TERMS, 1.0, 0.0)
    for h in range(n_heads):
        col = jnp.sum(jnp.where(lane == h, c, 0.0), axis=1, keepdims=True) * LOG2E
        fq = ones_lanes
        fk = ones_lanes
        for i, t in enumerate(_split3(col)):
            t32 = t.astype(F32)
            fq = jnp.where(lane == i, t32, fq)
            fk = jnp.where(lane == N_BIAS_TERMS + i, -t32, fk)
        fq_ref[:, h * LANES:(h + 1) * LANES] = fq.astype(BF16)
        fk_ref[:, h * LANES:(h + 1) * LANES] = fk.astype(BF16)


def forget_cumsum(zf, b_forget_row, tri, seq_len, n_heads):
    n = zf.shape[0]
    b = n // seq_len
    nt = seq_len // LANES
    w = n_heads * LANES
    return pl.pallas_call(
        functools.partial(_logf_kernel, n_heads=n_heads),
        out_shape=(jax.ShapeDtypeStruct((n, LANES), F32), jax.ShapeDtypeStruct((n, LANES), F32),
                   jax.ShapeDtypeStruct((b, SUBLANES, seq_len), F32),
                   jax.ShapeDtypeStruct((n, w), BF16), jax.ShapeDtypeStruct((n, w), BF16)),
        grid=(b, nt),
        in_specs=[
            pl.BlockSpec((LANES, LANES), lambda bi, ti: (bi * nt + ti, 0)),
            pl.BlockSpec((1, LANES), lambda bi, ti: (0, 0)),
            pl.BlockSpec((LANES, LANES), lambda bi, ti: (0, 0)),
        ],
        out_specs=(pl.BlockSpec((LANES, LANES), lambda bi, ti: (bi * nt + ti, 0)),
                   pl.BlockSpec((LANES, LANES), lambda bi, ti: (bi * nt + ti, 0)),
                   pl.BlockSpec((None, SUBLANES, LANES), lambda bi, ti: (bi, 0, ti)),
                   pl.BlockSpec((LANES, w), lambda bi, ti: (bi * nt + ti, 0)),
                   pl.BlockSpec((LANES, w), lambda bi, ti: (bi * nt + ti, 0))),
        scratch_shapes=[pltpu.VMEM((1, LANES), F32)],
        compiler_params=_cparams(("arbitrary", "arbitrary")),
        name="forget_cumsum",
    )(zf, b_forget_row, tri)


FOX_COL_BLOCK = 2 * HEAD_DIM


def _fox_kernel(qi_ref, ki_ref, *refs, tq, tk, n_heads):
    nb = n_heads * HEAD_DIM // FOX_COL_BLOCK
    q_refs, k_refs, v_refs = refs[0:nb], refs[nb:2 * nb], refs[2 * nb:3 * nb]
    fq_ref, fk_ref, o_ref, m_scr, l_scr, acc_scr, qa_scr = refs[3 * nb:]
    step = pl.program_id(1)
    qi = qi_ref[step]
    ki = ki_ref[step]

    def head_cols(block_refs, h):
        lo = (h * HEAD_DIM) % FOX_COL_BLOCK
        return block_refs[h * HEAD_DIM // FOX_COL_BLOCK][:, lo:lo + HEAD_DIM]

    @pl.when(ki == 0)
    def _():
        m_scr[...] = jnp.full_like(m_scr, NEG_INF)
        l_scr[...] = jnp.zeros_like(l_scr)
        acc_scr[...] = jnp.zeros_like(acc_scr)
        for h in range(n_heads):
            qa_scr[h, :, 0:HEAD_DIM] = (head_cols(q_refs, h) * (HEAD_DIM ** -0.5 * LOG2E)).astype(BF16)
            qa_scr[h, :, HEAD_DIM:] = fq_ref[:, h * LANES:(h + 1) * LANES]

    def update(h, masked):
        ka = jnp.concatenate([head_cols(k_refs, h).astype(BF16), fk_ref[:, h * LANES:(h + 1) * LANES]], axis=1)
        vb = head_cols(v_refs, h).astype(BF16)
        s = _dot_nt(qa_scr[h], ka)
        if masked:
            row = lax.broadcasted_iota(jnp.int32, (tq, tk), 0)
            col = lax.broadcasted_iota(jnp.int32, (tq, tk), 1)
            s = jnp.where(col <= row, s, NEG_INF)
        m_prev = m_scr[h]
        m_new = jnp.maximum(m_prev, jnp.max(s, axis=1, keepdims=True))
        alpha = jnp.exp2(m_prev - m_new)
        p = jnp.exp2(s - m_new[:, 0:1])
        l_scr[h] = alpha * l_scr[h] + jnp.sum(p, axis=1, keepdims=True)
        acc_scr[h] = alpha * acc_scr[h] + _dot(p.astype(BF16), vb)
        m_scr[h] = m_new

    @pl.when(ki < qi)
    def _():
        for h in range(n_heads):
            update(h, False)

    @pl.when(ki == qi)
    def _():
        for h in range(n_heads):
            update(h, True)
            o_ref[:, h * HEAD_DIM:(h + 1) * HEAD_DIM] = acc_scr[h] / l_scr[h]


def fox_attention(z, col_q, col_k, col_v, fq, fk, n_heads, seq_len, tq=512):
    n = z.shape[0]
    b = n // seq_len
    tk = tq
    nq = seq_len // tq
    hd = HEAD_DIM
    w = n_heads * hd
    nb = w // FOX_COL_BLOCK
    pairs = [(qi, ki) for qi in range(nq) for ki in range(qi + 1)]
    qi_tab = jnp.asarray([p[0] for p in pairs], jnp.int32)
    ki_tab = jnp.asarray([p[1] for p in pairs], jnp.int32)

    def cols(col0, use_q):
        j0 = col0 // FOX_COL_BLOCK
        tile = tq if use_q else tk
        if use_q:
            return [pl.BlockSpec((tile, FOX_COL_BLOCK), lambda bi, s, qt, kt, j=j0 + i: (bi * nq + qt[s], j))
                    for i in range(nb)]
        return [pl.BlockSpec((tile, FOX_COL_BLOCK), lambda bi, s, qt, kt, j=j0 + i: (bi * nq + kt[s], j))
                for i in range(nb)]

    grid_spec = pltpu.PrefetchScalarGridSpec(
        num_scalar_prefetch=2,
        grid=(b, len(pairs)),
        in_specs=cols(col_q, True) + cols(col_k, False) + cols(col_v, False) + [
            pl.BlockSpec((tq, w), lambda bi, s, qt, kt: (bi * nq + qt[s], 0)),
            pl.BlockSpec((tk, w), lambda bi, s, qt, kt: (bi * nq + kt[s], 0)),
        ],
        out_specs=pl.BlockSpec((tq, w), lambda bi, s, qt, kt: (bi * nq + qt[s], 0)),
        scratch_shapes=[pltpu.VMEM((n_heads, tq, LANES), F32), pltpu.VMEM((n_heads, tq, LANES), F32),
                        pltpu.VMEM((n_heads, tq, hd), F32), pltpu.VMEM((n_heads, tq, hd + LANES), BF16)],
    )
    return pl.pallas_call(
        functools.partial(_fox_kernel, tq=tq, tk=tk, n_heads=n_heads),
        out_shape=jax.ShapeDtypeStruct((n, w), F32),
        grid_spec=grid_spec,
        compiler_params=_cparams(("arbitrary", "arbitrary")),
        name="fox_attention",
    )(qi_tab, ki_tab, *([z] * (3 * nb)), fq, fk)


def _fox_paged_kernel(pt_ref, *refs, n_group, n_heads, t_pad, n_steps):
    del pt_ref
    g_ = n_group
    k_refs = refs[0:g_]
    v_refs = refs[g_:2 * g_]
    lf_refs = refs[2 * g_:3 * g_]
    (q_ref, kn_ref, vn_ref, cn_ref, ctn_ref, tri_ref, ones_ref, o_ref,
     m_scr, l_scr, acc_scr, carry_scr) = refs[3 * g_:]
    g = pl.program_id(1)
    scale = HEAD_DIM ** -0.5

    @pl.when(g == 0)
    def _():
        m_scr[...] = jnp.full_like(m_scr, NEG_INF)
        l_scr[...] = jnp.zeros_like(l_scr)
        acc_scr[...] = jnp.zeros_like(acc_scr)
        carry_scr[...] = jnp.zeros_like(carry_scr)

    lf_all = jnp.concatenate([r[...] for r in lf_refs], axis=0)
    hi, mid, lo = _split3(lf_all)
    tri = tri_ref[...]
    ones = ones_ref[...]
    after = _dot(hi, tri) + _dot(mid, tri) + _dot(lo, tri)
    total = _dot(hi, ones) + _dot(mid, ones) + _dot(lo, ones)
    carry = carry_scr[...]
    page_bias = []
    for i in range(g_):
        page_bias.append(carry + after[i * SUBLANES:(i + 1) * SUBLANES, :])
        carry = carry + total[i * SUBLANES:(i + 1) * SUBLANES, :]
    carry_scr[...] = carry

    cn = cn_ref[...]
    lane = lax.broadcasted_iota(jnp.int32, cn.shape, 1)

    def flash_update(h, s, v_list):
        m_prev = m_scr[h]
        m_new = jnp.maximum(m_prev, jnp.max(s, axis=1, keepdims=True))
        alpha = jnp.exp(m_prev - m_new)
        p32 = jnp.exp(s - m_new[:, 0:1])
        l_scr[h] = alpha * l_scr[h] + jnp.sum(p32, axis=1, keepdims=True)
        p = p32.astype(BF16)
        pv = _dot(p[:, 0:LANES], v_list[0])
        for i in range(1, len(v_list)):
            pv = pv + _dot(p[:, i * LANES:(i + 1) * LANES], v_list[i])
        acc_scr[h] = alpha * acc_scr[h] + pv
        m_scr[h] = m_new

    def scores(h):
        qh = q_ref[:, h * HEAD_DIM:(h + 1) * HEAD_DIM].astype(BF16)
        cq = jnp.sum(jnp.where(lane == h, cn, 0.0), axis=1, keepdims=True)
        s_list = []
        for i in range(g_):
            kh = k_refs[i][h].astype(BF16)
            s_list.append(_dot_nt(qh, kh) * scale + (cq + page_bias[i][h:h + 1, :]))
        return jnp.concatenate(s_list, axis=1)

    s_all = [scores(h) for h in range(n_heads)]
    for h in range(n_heads):
        flash_update(h, s_all[h], [v_refs[i][h].astype(BF16) for i in range(g_)])

    @pl.when(g == n_steps - 1)
    def _():
        row = lax.broadcasted_iota(jnp.int32, (t_pad, LANES), 0)
        col = lax.broadcasted_iota(jnp.int32, (t_pad, LANES), 1)
        for h in range(n_heads):
            hs = slice(h * HEAD_DIM, (h + 1) * HEAD_DIM)
            qh = q_ref[:, hs].astype(BF16)
            cq = jnp.sum(jnp.where(lane == h, cn, 0.0), axis=1, keepdims=True)
            s = _dot_nt(qh, kn_ref[:, hs].astype(BF16)) * scale + (cq - ctn_ref[h:h + 1, :])
            s = jnp.where(col <= row, s, NEG_INF)
            flash_update(h, s, [vn_ref[:, hs].astype(BF16)])
            o_ref[:, hs] = acc_scr[h] / l_scr[h]


def fox_paged_attention(q, k_new, v_new, c_new, ct_new, cache_k, cache_v, logf_t, page_table, layer,
                        tri_after, ones_mat, n_group=16, t_pad=SUBLANES):
    b, n_pages = page_table.shape
    n_heads = cache_k.shape[2]
    page = cache_k.shape[3]
    hd = HEAD_DIM
    w = n_heads * hd
    ng = n_pages // n_group

    def page_of(i):
        return lambda bi, gi, pt: (pt[bi, n_pages - 1 - (gi * n_group + i)], layer, 0, 0, 0)

    def lf_of(i):
        return lambda bi, gi, pt: (layer, pt[bi, n_pages - 1 - (gi * n_group + i)], 0, 0)

    kv_specs = [pl.BlockSpec((None, None, n_heads, page, hd), page_of(i)) for i in range(n_group)]
    lf_specs = [pl.BlockSpec((None, None, SUBLANES, page), lf_of(i)) for i in range(n_group)]
    grid_spec = pltpu.PrefetchScalarGridSpec(
        num_scalar_prefetch=1,
        grid=(b, ng),
        in_specs=kv_specs + kv_specs + lf_specs + [
            pl.BlockSpec((None, t_pad, w), lambda bi, gi, pt: (bi, 0, 0)),
            pl.BlockSpec((None, LANES, w), lambda bi, gi, pt: (bi, 0, 0)),
            pl.BlockSpec((None, LANES, w), lambda bi, gi, pt: (bi, 0, 0)),
            pl.BlockSpec((t_pad, LANES), lambda bi, gi, pt: (bi * (LANES // t_pad), 0)),
            pl.BlockSpec((None, SUBLANES, LANES), lambda bi, gi, pt: (bi, 0, 0)),
            pl.BlockSpec((LANES, LANES), lambda bi, gi, pt: (0, 0)),
            pl.BlockSpec((LANES, LANES), lambda bi, gi, pt: (0, 0)),
        ],
        out_specs=pl.BlockSpec((None, t_pad, w), lambda bi, gi, pt: (bi, 0, 0)),
        scratch_shapes=[pltpu.VMEM((n_heads, t_pad, LANES), F32), pltpu.VMEM((n_heads, t_pad, LANES), F32),
                        pltpu.VMEM((n_heads, t_pad, hd), F32), pltpu.VMEM((SUBLANES, LANES), F32)],
    )
    return pl.pallas_call(
        functools.partial(_fox_paged_kernel, n_group=n_group, n_heads=n_heads, t_pad=t_pad, n_steps=ng),
        out_shape=jax.ShapeDtypeStruct((b, t_pad, w), F32),
        grid_spec=grid_spec,
        compiler_params=_cparams(("arbitrary", "arbitrary")),
        name="fox_paged_attention",
    )(page_table, *([cache_k] * n_group), *([cache_v] * n_group), *([logf_t] * n_group),
      q, k_new, v_new, c_new, ct_new, tri_after, ones_mat)


def _router_kernel(x_ref, sc_ref, sh_ref, wr_ref, br_ref, h_ref, info_ref, *, n_experts):
    h = x_ref[...] * (1.0 + sc_ref[...]) + sh_ref[...]
    h_ref[...] = h
    logits = _dot(h.astype(BF16), wr_ref[...].astype(BF16)) + br_ref[...]
    lane = lax.broadcasted_iota(jnp.int32, logits.shape, 1)
    neg = -jnp.inf
    l1 = jnp.where(lane < n_experts, logits, neg)
    m1 = jnp.max(l1, axis=1, keepdims=True)
    i1 = jnp.min(jnp.where(l1 == m1, lane, LANES), axis=1, keepdims=True)
    l2 = jnp.where(lane == i1, neg, l1)
    m2 = jnp.max(l2, axis=1, keepdims=True)
    i2 = jnp.min(jnp.where(l2 == m2, lane, LANES), axis=1, keepdims=True)
    e = jnp.exp(m2 - m1)
    den = 1.0 + e
    p1 = 1.0 / den
    p2 = e / den
    info = jnp.where(lane == 0, i1.astype(F32),
                     jnp.where(lane == 1, i2.astype(F32),
                               jnp.where(lane == 2, p1, jnp.where(lane == 3, p2, 0.0))))
    info_ref[...] = info


def moe_route(x, sc, sh, w_router_pad, b_router_pad, n_experts, tm, rows_per_seq):
    n, d = x.shape
    return pl.pallas_call(
        functools.partial(_router_kernel, n_experts=n_experts),
        out_shape=(jax.ShapeDtypeStruct((n, d), F32), jax.ShapeDtypeStruct((n, LANES), F32)),
        grid=(n // tm,),
        in_specs=[
            pl.BlockSpec((tm, d), lambda i: (i, 0)),
            _mod_spec(sc, tm, rows_per_seq, 1),
            _mod_spec(sh, tm, rows_per_seq, 1),
            pl.BlockSpec((d, LANES), lambda i: (0, 0)),
            pl.BlockSpec((1, LANES), lambda i: (0, 0)),
        ],
        out_specs=(pl.BlockSpec((tm, d), lambda i: (i, 0)), pl.BlockSpec((tm, LANES), lambda i: (i, 0))),
        compiler_params=_cparams(("arbitrary",)),
        name="moe_router",
    )(x, sc, sh, w_router_pad, b_router_pad)


def moe_plan(experts, n_experts, tm):
    n = experts.shape[0]
    na = n * TOP_K
    n_tiles = pl.cdiv(na, tm) + n_experts
    flat_e = experts.reshape(na)
    onehot = (flat_e[:, None] == jnp.arange(n_experts, dtype=jnp.int32)[None, :]).astype(jnp.int32)
    incl = jnp.cumsum(onehot, axis=0)
    rank = jnp.sum((incl - onehot) * onehot, axis=1)
    counts = incl[-1]
    tiles_e = (counts + tm - 1) // tm
    tile_end = jnp.cumsum(tiles_e)
    tile_off = tile_end - tiles_e
    n_used = tile_end[-1:]
    dest = (jnp.sum(tile_off[None, :] * onehot, axis=1) * tm + rank).astype(jnp.int32)
    src_token = jnp.zeros((n_tiles * tm,), jnp.int32).at[dest].set(jnp.arange(na, dtype=jnp.int32) // TOP_K)
    t = jnp.arange(n_tiles, dtype=jnp.int32)
    tile_expert = jnp.sum((t[:, None] >= tile_end[None, :]).astype(jnp.int32), axis=1)
    tile_expert = jnp.minimum(tile_expert, n_experts - 1).astype(jnp.int32)
    return dest, src_token, tile_expert, n_used.astype(jnp.int32)


def _row_copy(src_ref, row, dst_ref, slot, sem):
    return pltpu.make_async_copy(src_ref.at[pl.ds(row, 1), :], dst_ref.at[pl.ds(slot, 1), :], sem)


GATHER_UNROLL = 8


def _moe_gather_kernel(src_ref, nused_ref, h_ref, o_ref, buf, sem, *, tm):
    t = pl.program_id(0)

    @pl.when(t < nused_ref[0])
    def _():
        base = t * tm

        def issue(r, carry):
            _row_copy(h_ref, src_ref[base + r], buf, r, sem).start()
            return carry

        lax.fori_loop(0, tm, issue, 0, unroll=GATHER_UNROLL)

        def drain(r, carry):
            _row_copy(h_ref, 0, buf, r, sem).wait()
            return carry

        lax.fori_loop(0, tm, drain, 0, unroll=GATHER_UNROLL)
        o_ref[...] = buf[...].astype(BF16)

    @pl.when(t >= nused_ref[0])
    def _():
        o_ref[...] = jnp.zeros_like(o_ref)


def moe_gather(h, src_token, n_used, tm):
    s = src_token.shape[0]
    d = h.shape[1]
    n_tiles = s // tm
    grid_spec = pltpu.PrefetchScalarGridSpec(
        num_scalar_prefetch=2,
        grid=(n_tiles,),
        in_specs=[pl.BlockSpec(memory_space=pl.ANY)],
        out_specs=pl.BlockSpec((tm, d), lambda t, src, nu: (t, 0)),
        scratch_shapes=[pltpu.VMEM((tm, d), F32), pltpu.SemaphoreType.DMA(())],
    )
    return pl.pallas_call(
        functools.partial(_moe_gather_kernel, tm=tm),
        out_shape=jax.ShapeDtypeStruct((s, d), BF16),
        grid_spec=grid_spec,
        compiler_params=_cparams(("arbitrary",)),
        name="moe_gather",
    )(src_token, n_used, h)


def _moe_up_kernel(te_ref, nused_ref, x_ref, wg_ref, wu_ref, a_ref):
    del te_ref

    @pl.when(pl.program_id(1) < nused_ref[0])
    def _():
        xb = x_ref[...]
        gate = _dot(xb, wg_ref[...].astype(BF16))
        up = _dot(xb, wu_ref[...].astype(BF16))
        a_ref[...] = (_silu(gate) * up).astype(a_ref.dtype)

    @pl.when(pl.program_id(1) >= nused_ref[0])
    def _():
        a_ref[...] = jnp.zeros_like(a_ref)


def moe_up(xs, w_gate, w_up, idx, tile_expert, n_used, tm, tf=512):
    s, d = xs.shape
    ff = w_gate.shape[3]
    n_tiles = s // tm

    def tile(t, nu):
        return jnp.minimum(t, nu[0] - 1)

    grid_spec = pltpu.PrefetchScalarGridSpec(
        num_scalar_prefetch=2,
        grid=(pl.cdiv(ff, tf), n_tiles),
        in_specs=[
            pl.BlockSpec((tm, d), lambda f, t, te, nu: (tile(t, nu), 0)),
            pl.BlockSpec((None, None, d, tf), lambda f, t, te, nu: (idx, te[tile(t, nu)], 0, f)),
            pl.BlockSpec((None, None, d, tf), lambda f, t, te, nu: (idx, te[tile(t, nu)], 0, f)),
        ],
        out_specs=pl.BlockSpec((tm, tf), lambda f, t, te, nu: (t, f)),
    )
    return pl.pallas_call(
        _moe_up_kernel,
        out_shape=jax.ShapeDtypeStruct((s, ff), BF16),
        grid_spec=grid_spec,
        compiler_params=_cparams(("arbitrary", "arbitrary")),
        name="moe_up",
    )(tile_expert, n_used, xs, w_gate, w_up)


def _moe_down_kernel(te_ref, nused_ref, a_ref, w_ref, o_ref):
    del te_ref

    @pl.when(pl.program_id(1) < nused_ref[0])
    def _():
        o_ref[...] = _dot(a_ref[...], w_ref[...].astype(BF16))

    @pl.when(pl.program_id(1) >= nused_ref[0])
    def _():
        o_ref[...] = jnp.zeros_like(o_ref)


def moe_down(a, w_down, idx, tile_expert, n_used, tm, tn=512):
    s, ff = a.shape
    d = w_down.shape[3]
    n_tiles = s // tm

    def tile(t, nu):
        return jnp.minimum(t, nu[0] - 1)

    grid_spec = pltpu.PrefetchScalarGridSpec(
        num_scalar_prefetch=2,
        grid=(d // tn, n_tiles),
        in_specs=[
            pl.BlockSpec((tm, ff), lambda j, t, te, nu: (tile(t, nu), 0)),
            pl.BlockSpec((None, None, ff, tn), lambda j, t, te, nu: (idx, te[tile(t, nu)], 0, j)),
        ],
        out_specs=pl.BlockSpec((tm, tn), lambda j, t, te, nu: (t, j)),
    )
    return pl.pallas_call(
        _moe_down_kernel,
        out_shape=jax.ShapeDtypeStruct((s, d), F32),
        grid_spec=grid_spec,
        compiler_params=_cparams(("arbitrary", "arbitrary")),
        name="moe_down",
    )(tile_expert, n_used, a, w_down)


def _moe_combine_kernel(dest_ref, y_ref, info_ref, x_ref, g_ref, w_ref, b_ref, o_ref, buf0, buf1, sem,
                        *, tm, row0, alpha):
    base = (row0 + pl.program_id(0) * tm) * TOP_K

    def issue(r, carry):
        _row_copy(y_ref, dest_ref[base + TOP_K * r], buf0, r, sem).start()
        _row_copy(y_ref, dest_ref[base + TOP_K * r + 1], buf1, r, sem).start()
        return carry

    lax.fori_loop(0, tm, issue, 0, unroll=GATHER_UNROLL)

    def drain(r, carry):
        _row_copy(y_ref, 0, buf0, r, sem).wait()
        _row_copy(y_ref, 0, buf1, r, sem).wait()
        return carry

    lax.fori_loop(0, tm, drain, 0, unroll=GATHER_UNROLL)
    info = info_ref[...]
    f = info[:, 2:3] * buf0[...] + info[:, 3:4] * buf1[...]
    y = alpha * x_ref[...] + g_ref[...] * f
    o_ref[...] = _layer_norm_rows(y, w_ref[...], b_ref[...])


def moe_combine_ln(y_sorted, dest, info, x, g, w, b, row0, alpha, tm, rows_per_seq):
    n, d = x.shape
    gspec = _mod_spec(g, tm, rows_per_seq, 1)
    gmap = gspec.index_map
    grid_spec = pltpu.PrefetchScalarGridSpec(
        num_scalar_prefetch=1,
        grid=(n // tm,),
        in_specs=[
            pl.BlockSpec(memory_space=pl.ANY),
            pl.BlockSpec((tm, LANES), lambda i, dst: (i, 0)),
            pl.BlockSpec((tm, d), lambda i, dst: (i, 0)),
            pl.BlockSpec(gspec.block_shape, lambda i, dst: gmap(i)),
            pl.BlockSpec((1, d), lambda i, dst: (0, 0)),
            pl.BlockSpec((1, d), lambda i, dst: (0, 0)),
        ],
        out_specs=pl.BlockSpec((tm, d), lambda i, dst: (i, 0)),
        scratch_shapes=[pltpu.VMEM((tm, d), F32), pltpu.VMEM((tm, d), F32), pltpu.SemaphoreType.DMA(())],
    )
    return pl.pallas_call(
        functools.partial(_moe_combine_kernel, tm=tm, row0=row0, alpha=alpha),
        out_shape=jax.ShapeDtypeStruct((n, d), F32),
        grid_spec=grid_spec,
        compiler_params=_cparams(("arbitrary",)),
        name="moe_combine_ln",
    )(dest, y_sorted, info, x, g, w.reshape(1, d), b.reshape(1, d))


PROMPT_TM = 1024
PROMPT_LN_TM = 512
MOE_TM = 512
MOE_COMBINE_TM = 256


def _pad_axis(a, axis, size):
    pad = [(0, 0)] * a.ndim
    pad[axis] = (0, size - a.shape[axis])
    return jnp.pad(a, pad)


def kernel(x_prompt, x_sample, cache_k, cache_v, cache_logf, state_ret, state_conv, page_table, c_prompt, c_sample, w_ada, b_ada, w_in, b_forget, ret_gn_w, conv_w, conv_b, conv_ln_w, conv_ln_b, w_out, ln1_w, ln1_b, ln2_w, ln2_b, ffn_w_gate, ffn_w_up, ffn_w_down, moe_router, moe_router_b, moe_w_gate, moe_w_up, moe_w_down):
    bp, tp, d = x_prompt.shape
    bs, ts, _ = x_sample.shape
    depth = w_in.shape[0]
    ret_heads = state_ret.shape[2]
    fox_heads = cache_k.shape[3]
    page = cache_k.shape[2]
    conv_dim = state_conv.shape[3]
    n_experts = moe_router.shape[2]
    hd = HEAD_DIM
    ret_w, fox_w = ret_heads * hd, fox_heads * hd
    o_ca = 4 * ret_w
    o_cb = o_ca + conv_dim
    o_fq = o_cb + conv_dim
    o_fk = o_fq + fox_w
    o_fv = o_fk + fox_w
    n_main = o_fv + fox_w
    alpha = (2 * depth) ** 0.25
    past_len = page_table.shape[1] * page
    tpad = RET_CHUNK
    np_rows, ns_rows = bp * tp, bs * ts

    w_forget = _pad_axis(w_in[:, :, n_main:], 2, LANES)
    bf_rows = _pad_axis(b_forget, 1, LANES).reshape(depth, 1, LANES)
    router_w = _pad_axis(moe_router, 2, LANES)
    router_b = _pad_axis(moe_router_b, 1, LANES).reshape(-1, 1, LANES)
    conv_w_pad = _pad_axis(conv_w, 1, CONV_HALO)
    logf_t = _pad_axis(jnp.transpose(cache_logf, (1, 0, 3, 2)), 2, SUBLANES)
    cache_k_hm = jnp.transpose(cache_k, (0, 1, 3, 2, 4))
    cache_v_hm = jnp.transpose(cache_v, (0, 1, 3, 2, 4))
    tri_incl = jnp.tril(jnp.ones((LANES, LANES), BF16))
    tri_after = jnp.tril(jnp.ones((page, page), BF16), -1)
    ones_mat = jnp.ones((page, page), BF16)
    tabs_p = retention_tables(ret_heads, tp, 0, tp)
    tabs_s = retention_tables(ret_heads, ts, past_len, tpad)
    r0_p = jnp.zeros((bp, ret_heads, hd, hd), F32)
    buf0_p = jnp.zeros((bp, CONV_HALO, conv_dim), F32)

    c_rows = _pad_axis(jnp.concatenate([c_prompt, c_sample], axis=0), 0, 2 * SUBLANES)
    mod = ada_modulation(c_rows, w_ada, b_ada)

    xp = x_prompt.reshape(np_rows, d)
    xs = x_sample.reshape(ns_rows, d)
    rets_p, rets_s, convs_p, convs_s, ks_p, ks_s, vs_p, vs_s, lfs_p, lfs_s = ([] for _ in range(10))

    def unpad(a):
        return a.reshape(bs, tpad, -1)[:, :ts].reshape(ns_rows, -1)

    for l in range(depth):
        parts = [mod[l, :, i * d:(i + 1) * d] for i in range(6)]
        sh1p, sc1p, g1p, sh2p, sc2p, g2p = (p[:bp].reshape(bp, 1, d) for p in parts)
        sh1s, sc1s, g1s, sh2s, sc2s, g2s = (jnp.repeat(p[bp:bp + bs], ts, axis=0) for p in parts)

        z, zf = proj_in(xp, sc1p, sh1p, w_in, w_forget, l, n_main, PROMPT_TM, tp, tn=640)
        ret_o, r_new = retention(z, tabs_p, ret_gn_w[l], r0_p, ret_heads, tp)
        conv_o, conv_new = conv_branch(z, o_ca, o_cb, buf0_p, conv_w_pad[l], conv_b[l], conv_ln_w[l],
                                       conv_ln_b[l], tp, RET_CHUNK)
        lf, _, _, bias_q, bias_k = forget_cumsum(zf, bf_rows[l], tri_incl, tp, fox_heads)
        fox_o = fox_attention(z, o_fq, o_fk, o_fv, bias_q, bias_k, fox_heads, tp)
        m = proj_out(ret_o, conv_o, fox_o, w_out, l, PROMPT_TM)
        x1p = ln_residual(xp, m, g1p, ln1_w[l], ln1_b[l], alpha, PROMPT_LN_TM, tp)
        rets_p.append(r_new)
        convs_p.append(conv_new[:, CONV_HALO - (CONV_W - 1):])
        ks_p.append(z[:, o_fk:o_fv].reshape(bp, tp, fox_heads, hd))
        vs_p.append(z[:, o_fv:n_main].reshape(bp, tp, fox_heads, hd))
        lfs_p.append(lf[:, :fox_heads].reshape(bp, tp, fox_heads))

        zs, zfs = proj_in(xs, sc1s, sh1s, w_in, w_forget, l, n_main, ns_rows, ts, tn=640)
        zs_pad = _pad_axis(zs.reshape(bs, ts, n_main), 1, tpad)
        z2 = zs_pad.reshape(bs * tpad, n_main)
        zf2 = _pad_axis(zfs.reshape(bs, ts, LANES), 1, tpad).reshape(bs * tpad, LANES)
        ret_s, r_new_s = retention(z2, tabs_s, ret_gn_w[l], state_ret[:, l], ret_heads, tpad)
        buf_s = jnp.pad(state_conv[:, l], ((0, 0), (CONV_HALO - (CONV_W - 1), 0), (0, 0)))
        conv_s, conv_new_s = conv_branch(z2, o_ca, o_cb, buf_s, conv_w_pad[l], conv_b[l], conv_ln_w[l],
                                         conv_ln_b[l], tpad, ts)
        lf_s, csum_s, csum_t_s, _, _ = forget_cumsum(zf2, bf_rows[l], tri_incl, tpad, fox_heads)
        fox_s = fox_paged_attention(zs_pad[:, :SUBLANES, o_fq:o_fk], zs_pad[:, :, o_fk:o_fv],
                                    zs_pad[:, :, o_fv:n_main], csum_s, csum_t_s, cache_k_hm, cache_v_hm, logf_t,
                                    page_table, l, tri_after, ones_mat)
        fox_s = fox_s[:, :ts].reshape(ns_rows, fox_w)
        m_s = proj_out(unpad(ret_s), unpad(conv_s), fox_s, w_out, l, ns_rows)
        x1s = ln_residual(xs, m_s, g1s, ln1_w[l], ln1_b[l], alpha, ns_rows, ts)
        rets_s.append(r_new_s)
        convs_s.append(conv_new_s[:, CONV_HALO - (CONV_W - 1):])
        ks_s.append(zs[:, o_fk:o_fv].reshape(bs, ts, fox_heads, hd))
        vs_s.append(zs[:, o_fv:n_main].reshape(bs, ts, fox_heads, hd))
        lfs_s.append(unpad(lf_s)[:, :fox_heads].reshape(bs, ts, fox_heads))

        i = l // 2
        if l % 2 == 0:
            a = ffn_up(x1p, sc2p, sh2p, ffn_w_gate, ffn_w_up, i, PROMPT_TM, tp)
            y = ffn_down(a, ffn_w_down, i, PROMPT_TM)
            xp = ln_residual(x1p, y, g2p, ln2_w[l], ln2_b[l], alpha, PROMPT_LN_TM, tp)
            a_s = ffn_up(x1s, sc2s, sh2s, ffn_w_gate, ffn_w_up, i, ns_rows, ts)
            y_s = ffn_down(a_s, ffn_w_down, i, ns_rows)
            xs = ln_residual(x1s, y_s, g2s, ln2_w[l], ln2_b[l], alpha, ns_rows, ts)
        else:
            h_p, info_p = moe_route(x1p, sc2p, sh2p, router_w[i], router_b[i], n_experts, PROMPT_LN_TM, tp)
            h_s, info_s = moe_route(x1s, sc2s, sh2s, router_w[i], router_b[i], n_experts, ns_rows, ts)
            experts = jnp.concatenate([info_p[:, :TOP_K], info_s[:, :TOP_K]], axis=0).astype(jnp.int32)
            dest, src_token, tile_expert, n_used = moe_plan(experts, n_experts, MOE_TM)
            x_sorted = moe_gather(jnp.concatenate([h_p, h_s], axis=0), src_token, n_used, MOE_TM)
            a = moe_up(x_sorted, moe_w_gate, moe_w_up, i, tile_expert, n_used, MOE_TM)
            y = moe_down(a, moe_w_down, i, tile_expert, n_used, MOE_TM)
            xp = moe_combine_ln(y, dest, info_p, x1p, g2p, ln2_w[l], ln2_b[l], 0, alpha, MOE_COMBINE_TM, tp)
            xs = moe_combine_ln(y, dest, info_s, x1s, g2s, ln2_w[l], ln2_b[l], np_rows, alpha, ns_rows, ts)

    st = lambda lst: jnp.stack(lst, axis=1)
    return (xp.reshape(bp, tp, d), xs.reshape(bs, ts, d), st(rets_p), st(rets_s), st(convs_p), st(convs_s),
            st(ks_p), st(ks_s), st(vs_p), st(vs_s), st(lfs_p), st(lfs_s))
```

```python
import functools

import jax
import jax.numpy as jnp
from jax import lax
from jax.experimental import pallas as pl
from jax.experimental.pallas import tpu as pltpu

F32 = jnp.float32
BF16 = jnp.bfloat16

LANES = 128
SUBLANES = 8
VMEM_LIMIT_BYTES = 56 * 1024 * 1024

HEAD_DIM = 128
CONV_W = 31
CONV_HALO = 32
RET_CHUNK = 128
TOP_K = 2
LN_EPS = 1e-5
NEG_INF = -1e30
ROPE_BASE = 10000.0


def _cparams(sem, vmem=VMEM_LIMIT_BYTES):
    return pltpu.CompilerParams(dimension_semantics=sem, vmem_limit_bytes=vmem)


def _silu(x):
    return x * jax.nn.sigmoid(x)


def _dot(a, b):
    return jnp.dot(a, b, preferred_element_type=F32)


def _dot_nt(a, b):
    return lax.dot_general(a, b, (((1,), (1,)), ((), ())), preferred_element_type=F32)


def _split3(x):
    hi = x.astype(BF16)
    r1 = x - hi.astype(F32)
    mid = r1.astype(BF16)
    lo = (r1 - mid.astype(F32)).astype(BF16)
    return hi, mid, lo


def _log_sigmoid(x):
    return -(jnp.maximum(-x, 0.0) + jnp.log1p(jnp.exp(-jnp.abs(x))))


def _layer_norm_rows(y, w, b):
    mu = jnp.mean(y, axis=-1, keepdims=True)
    d = y - mu
    var = jnp.mean(d * d, axis=-1, keepdims=True)
    return d * lax.rsqrt(var + LN_EPS) * w + b


def _ada_kernel(c_ref, w_ref, b_ref, o_ref):
    cs = _silu(c_ref[...]).astype(BF16)
    o_ref[...] = _dot(cs, w_ref[...].astype(BF16)) + b_ref[...]


def ada_modulation(c_rows, w_ada, b_ada, tn=1024):
    depth, d, n6 = w_ada.shape
    r = c_rows.shape[0]
    return pl.pallas_call(
        _ada_kernel,
        out_shape=jax.ShapeDtypeStruct((depth, r, n6), F32),
        grid=(depth, n6 // tn),
        in_specs=[
            pl.BlockSpec((r, d), lambda l, j: (0, 0)),
            pl.BlockSpec((None, d, tn), lambda l, j: (l, 0, j)),
            pl.BlockSpec((None, 1, tn), lambda l, j: (l, 0, j)),
        ],
        out_specs=pl.BlockSpec((None, r, tn), lambda l, j: (l, 0, j)),
        compiler_params=_cparams(("arbitrary", "arbitrary")),
        name="ada_modulation",
    )(c_rows, w_ada, b_ada.reshape(depth, 1, n6))


def _mod_spec(arr, tm, rows_per_seq, grid_rank):
    d = arr.shape[-1]
    if arr.ndim == 3:
        assert rows_per_seq % tm == 0
        per = rows_per_seq // tm
        if grid_rank == 1:
            return pl.BlockSpec((None, 1, d), lambda i: (i // per, 0, 0))
        return pl.BlockSpec((None, 1, d), lambda i, j: (i // per, 0, 0))
    if grid_rank == 1:
        return pl.BlockSpec((tm, d), lambda i: (i, 0))
    return pl.BlockSpec((tm, d), lambda i, j: (i, 0))


def _proj_in_kernel(x_ref, sc_ref, sh_ref, w_ref, wf_ref, z_ref, zf_ref, xb_ref):
    @pl.when(pl.program_id(1) == 0)
    def _():
        h = x_ref[...] * (1.0 + sc_ref[...]) + sh_ref[...]
        xb_ref[...] = h.astype(BF16)
        zf_ref[...] = _dot(xb_ref[...], wf_ref[...].astype(BF16))

    z_ref[...] = _dot(xb_ref[...], w_ref[...].astype(BF16))


def proj_in(x, sc, sh, w_in, w_forget, layer, n_main, tm, rows_per_seq, tn=512):
    n, d = x.shape
    return pl.pallas_call(
        _proj_in_kernel,
        out_shape=(jax.ShapeDtypeStruct((n, n_main), F32), jax.ShapeDtypeStruct((n, LANES), F32)),
        grid=(n // tm, n_main // tn),
        in_specs=[
            pl.BlockSpec((tm, d), lambda i, j: (i, 0)),
            _mod_spec(sc, tm, rows_per_seq, 2),
            _mod_spec(sh, tm, rows_per_seq, 2),
            pl.BlockSpec((None, d, tn), lambda i, j: (layer, 0, j)),
            pl.BlockSpec((None, d, LANES), lambda i, j: (layer, 0, 0)),
        ],
        out_specs=(pl.BlockSpec((tm, tn), lambda i, j: (i, j)),
                   pl.BlockSpec((tm, LANES), lambda i, j: (i, 0))),
        scratch_shapes=[pltpu.VMEM((tm, d), BF16)],
        compiler_params=_cparams(("arbitrary", "arbitrary")),
        name="proj_in",
    )(x, sc, sh, w_in, w_forget)


def _proj_out_kernel(a_ref, b_ref, c_ref, w_ref, o_ref, xb_ref):
    @pl.when(pl.program_id(1) == 0)
    def _():
        wa, wb = a_ref.shape[1], b_ref.shape[1]
        xb_ref[:, 0:wa] = a_ref[...].astype(BF16)
        xb_ref[:, wa:wa + wb] = b_ref[...].astype(BF16)
        xb_ref[:, wa + wb:] = c_ref[...].astype(BF16)

    o_ref[...] = _dot(xb_ref[...], w_ref[...].astype(BF16))


def proj_out(a, b, c, w_out, layer, tm, tn=512):
    n = a.shape[0]
    d = w_out.shape[1]
    dout = w_out.shape[2]
    return pl.pallas_call(
        _proj_out_kernel,
        out_shape=jax.ShapeDtypeStruct((n, dout), F32),
        grid=(n // tm, dout // tn),
        in_specs=[
            pl.BlockSpec((tm, a.shape[1]), lambda i, j: (i, 0)),
            pl.BlockSpec((tm, b.shape[1]), lambda i, j: (i, 0)),
            pl.BlockSpec((tm, c.shape[1]), lambda i, j: (i, 0)),
            pl.BlockSpec((None, d, tn), lambda i, j: (layer, 0, j)),
        ],
        out_specs=pl.BlockSpec((tm, tn), lambda i, j: (i, j)),
        scratch_shapes=[pltpu.VMEM((tm, d), BF16)],
        compiler_params=_cparams(("arbitrary", "arbitrary")),
        name="proj_out",
    )(a, b, c, w_out)


def _ln_res_kernel(x_ref, m_ref, g_ref, w_ref, b_ref, o_ref, *, alpha):
    y = alpha * x_ref[...] + g_ref[...] * m_ref[...]
    o_ref[...] = _layer_norm_rows(y, w_ref[...], b_ref[...])


def ln_residual(x, m, g, w, b, alpha, tm, rows_per_seq):
    n, d = x.shape
    return pl.pallas_call(
        functools.partial(_ln_res_kernel, alpha=alpha),
        out_shape=jax.ShapeDtypeStruct((n, d), F32),
        grid=(n // tm,),
        in_specs=[
            pl.BlockSpec((tm, d), lambda i: (i, 0)),
            pl.BlockSpec((tm, d), lambda i: (i, 0)),
            _mod_spec(g, tm, rows_per_seq, 1),
            pl.BlockSpec((1, d), lambda i: (0, 0)),
            pl.BlockSpec((1, d), lambda i: (0, 0)),
        ],
        out_specs=pl.BlockSpec((tm, d), lambda i: (i, 0)),
        compiler_params=_cparams(("arbitrary",)),
        name="ln_residual",
    )(x, m, g, w.reshape(1, d), b.reshape(1, d))


def _ffn_up_kernel(x_ref, sc_ref, sh_ref, wg_ref, wu_ref, a_ref, xb_ref):
    @pl.when(pl.program_id(1) == 0)
    def _():
        h = x_ref[...] * (1.0 + sc_ref[...]) + sh_ref[...]
        xb_ref[...] = h.astype(BF16)

    xb = xb_ref[...]
    gate = _dot(xb, wg_ref[...].astype(BF16))
    up = _dot(xb, wu_ref[...].astype(BF16))
    a_ref[...] = (_silu(gate) * up).astype(a_ref.dtype)


def ffn_up(x, sc, sh, w_gate, w_up, idx, tm, rows_per_seq, tf=512):
    n, d = x.shape
    ff = w_gate.shape[2]
    return pl.pallas_call(
        _ffn_up_kernel,
        out_shape=jax.ShapeDtypeStruct((n, ff), BF16),
        grid=(n // tm, pl.cdiv(ff, tf)),
        in_specs=[
            pl.BlockSpec((tm, d), lambda i, j: (i, 0)),
            _mod_spec(sc, tm, rows_per_seq, 2),
            _mod_spec(sh, tm, rows_per_seq, 2),
            pl.BlockSpec((None, d, tf), lambda i, j: (idx, 0, j)),
            pl.BlockSpec((None, d, tf), lambda i, j: (idx, 0, j)),
        ],
        out_specs=pl.BlockSpec((tm, tf), lambda i, j: (i, j)),
        scratch_shapes=[pltpu.VMEM((tm, d), BF16)],
        compiler_params=_cparams(("arbitrary", "arbitrary")),
        name="ffn_up",
    )(x, sc, sh, w_gate, w_up)


def _ffn_down_kernel(a_ref, w_ref, o_ref):
    o_ref[...] = _dot(a_ref[...], w_ref[...].astype(BF16))


def ffn_down(a, w_down, idx, tm, tn=256):
    n, ff = a.shape
    d = w_down.shape[2]
    return pl.pallas_call(
        _ffn_down_kernel,
        out_shape=jax.ShapeDtypeStruct((n, d), F32),
        grid=(n // tm, d // tn),
        in_specs=[
            pl.BlockSpec((tm, ff), lambda i, j: (i, 0)),
            pl.BlockSpec((None, ff, tn), lambda i, j: (idx, 0, j)),
        ],
        out_specs=pl.BlockSpec((tm, tn), lambda i, j: (i, j)),
        compiler_params=_cparams(("arbitrary", "arbitrary")),
        name="ffn_down",
    )(a, w_down)


def _retention_kernel(q_ref, k_ref, v_ref, g_ref, cos_ref, sin_ref, dmat_ref, xi_ref, zeta_ref,
                      gc_ref, gnw_ref, r0_ref, y_ref, rn_ref, r_scr, *, n_chunks, n_heads):
    c = pl.program_id(1)

    @pl.when(c == 0)
    def _():
        r_scr[...] = r0_ref[...]

    cos = cos_ref[...]
    sin = sin_ref[...]

    def rope(x):
        return x * cos + pltpu.roll(x, HEAD_DIM // 2, axis=1) * sin

    for h in range(n_heads):
        hs = slice(h * HEAD_DIM, (h + 1) * HEAD_DIM)
        q = rope(q_ref[:, hs])
        k = rope(k_ref[:, hs]) * (HEAD_DIM ** -0.5)
        qb = q.astype(BF16)
        kb = k.astype(BF16)
        vb = v_ref[:, hs].astype(BF16)
        r_prev = r_scr[h]

        s = _dot_nt(qb, kb) * dmat_ref[h]
        y = _dot(s.astype(BF16), vb) + _dot(qb, r_prev.astype(BF16)) * xi_ref[h]
        kz_t = jnp.transpose(k * zeta_ref[h]).astype(BF16)
        r_scr[h] = gc_ref[h] * r_prev + _dot(kz_t, vb)

        mu = jnp.mean(y, axis=-1, keepdims=True)
        d = y - mu
        var = jnp.mean(d * d, axis=-1, keepdims=True)
        yn = d * lax.rsqrt(var + LN_EPS) * gnw_ref[:, hs]
        y_ref[:, hs] = yn * _silu(g_ref[:, hs])

    @pl.when(c == n_chunks - 1)
    def _():
        rn_ref[...] = r_scr[...]


def retention(z, tabs, gn_w, r0, n_heads, seq_len):
    n = z.shape[0]
    b = n // seq_len
    cch = RET_CHUNK
    nc = seq_len // cch
    hd = HEAD_DIM
    h_ = n_heads
    w = h_ * hd
    const3 = lambda bi, ci: (0, 0, 0)
    return pl.pallas_call(
        functools.partial(_retention_kernel, n_chunks=nc, n_heads=h_),
        out_shape=(jax.ShapeDtypeStruct((n, w), F32), jax.ShapeDtypeStruct((b, h_, hd, hd), F32)),
        grid=(b, nc),
        in_specs=[
            pl.BlockSpec((cch, w), lambda bi, ci: (bi * nc + ci, 0)),
            pl.BlockSpec((cch, w), lambda bi, ci: (bi * nc + ci, 1)),
            pl.BlockSpec((cch, w), lambda bi, ci: (bi * nc + ci, 2)),
            pl.BlockSpec((cch, w), lambda bi, ci: (bi * nc + ci, 3)),
            pl.BlockSpec((cch, hd), lambda bi, ci: (ci, 0)),
            pl.BlockSpec((cch, hd), lambda bi, ci: (ci, 0)),
            pl.BlockSpec((h_, cch, cch), const3),
            pl.BlockSpec((h_, cch, hd), const3),
            pl.BlockSpec((h_, cch, hd), const3),
            pl.BlockSpec((h_, 1, hd), const3),
            pl.BlockSpec((1, w), lambda bi, ci: (0, 0)),
            pl.BlockSpec((None, h_, hd, hd), lambda bi, ci: (bi, 0, 0, 0)),
        ],
        out_specs=(pl.BlockSpec((cch, w), lambda bi, ci: (bi * nc + ci, 0)),
                   pl.BlockSpec((None, h_, hd, hd), lambda bi, ci: (bi, 0, 0, 0))),
        scratch_shapes=[pltpu.VMEM((h_, hd, hd), F32)],
        compiler_params=_cparams(("arbitrary", "arbitrary")),
        name="retention",
    )(z, z, z, z, tabs["cos"], tabs["sin"], tabs["dmat"], tabs["xi"], tabs["zeta"], tabs["gc"],
      gn_w.reshape(1, w), r0)


def retention_tables(n_heads, t_valid, pos0, t_pad):
    cch = min(t_valid, RET_CHUNK)
    half = HEAD_DIM // 2
    lg = jnp.log(1.0 - 2.0 ** (-5.0 - jnp.arange(n_heads, dtype=F32)))
    idx = jnp.arange(cch, dtype=F32)
    diff = idx[:, None] - idx[None, :]
    dmat = jnp.where(diff[None] >= 0, jnp.exp(jnp.maximum(diff, 0.0)[None] * lg[:, None, None]), 0.0)
    xi = jnp.exp((idx[None, :] + 1.0) * lg[:, None])
    zeta = jnp.exp((cch - 1.0 - idx)[None, :] * lg[:, None])
    gc = jnp.exp(cch * lg)
    pc = RET_CHUNK - cch
    dmat = jnp.pad(dmat, ((0, 0), (0, pc), (0, pc)))
    xi = jnp.broadcast_to(jnp.pad(xi, ((0, 0), (0, pc)))[:, :, None], (n_heads, RET_CHUNK, HEAD_DIM))
    zeta = jnp.broadcast_to(jnp.pad(zeta, ((0, 0), (0, pc)))[:, :, None], (n_heads, RET_CHUNK, HEAD_DIM))
    gc = jnp.broadcast_to(gc[:, None, None], (n_heads, 1, HEAD_DIM))
    pos = (pos0 + jnp.arange(t_pad)).astype(F32)
    inv = ROPE_BASE ** (-jnp.arange(half, dtype=F32) / half)
    ang = pos[:, None] * inv[None, :]
    cos = jnp.cos(ang)
    sin = jnp.sin(ang)
    return dict(dmat=dmat, xi=xi, zeta=zeta, gc=gc,
                cos=jnp.concatenate([cos, cos], axis=1), sin=jnp.concatenate([-sin, sin], axis=1))


CONV_ROWS = 32


def _conv_kernel(za_ref, zb_ref, buf_ref, cw_ref, cb_ref, lw_ref, lb_ref, o_ref, new_ref, u_scr,
                 *, tt, t_last, n_tiles):
    t = pl.program_id(1)

    @pl.when(t == 0)
    def _():
        u_scr[0:CONV_HALO, :] = buf_ref[...]

    @pl.when(t > 0)
    def _():
        u_scr[0:CONV_HALO, :] = u_scr[tt:tt + CONV_HALO, :]

    u_scr[CONV_HALO:CONV_HALO + tt, :] = za_ref[...] * jax.nn.sigmoid(zb_ref[...])

    off = CONV_HALO - (CONV_W - 1)
    for r0 in range(0, tt, CONV_ROWS):
        acc = u_scr[r0 + off:r0 + off + CONV_ROWS, :] * cw_ref[0:1, :]
        for w in range(1, CONV_W):
            acc = acc + u_scr[r0 + off + w:r0 + off + w + CONV_ROWS, :] * cw_ref[w:w + 1, :]
        y = _layer_norm_rows(acc + cb_ref[...], lw_ref[...], lb_ref[...])
        o_ref[r0:r0 + CONV_ROWS, :] = _silu(y)

    @pl.when(t == n_tiles - 1)
    def _():
        new_ref[...] = u_scr[t_last:t_last + CONV_HALO, :]


def conv_branch(z, col_a, col_b, conv_buf, cw, cb, lw, lb, seq_len, t_last, tt=128):
    n = z.shape[0]
    b = n // seq_len
    nt = seq_len // tt
    cdim = cw.shape[1]
    ja, jb = col_a // cdim, col_b // cdim
    return pl.pallas_call(
        functools.partial(_conv_kernel, tt=tt, t_last=t_last, n_tiles=nt),
        out_shape=(jax.ShapeDtypeStruct((n, cdim), F32), jax.ShapeDtypeStruct((b, CONV_HALO, cdim), F32)),
        grid=(b, nt),
        in_specs=[
            pl.BlockSpec((tt, cdim), lambda bi, ti: (bi * nt + ti, ja)),
            pl.BlockSpec((tt, cdim), lambda bi, ti: (bi * nt + ti, jb)),
            pl.BlockSpec((None, CONV_HALO, cdim), lambda bi, ti: (bi, 0, 0)),
            pl.BlockSpec((CONV_HALO, cdim), lambda bi, ti: (0, 0)),
            pl.BlockSpec((1, cdim), lambda bi, ti: (0, 0)),
            pl.BlockSpec((1, cdim), lambda bi, ti: (0, 0)),
            pl.BlockSpec((1, cdim), lambda bi, ti: (0, 0)),
        ],
        out_specs=(pl.BlockSpec((tt, cdim), lambda bi, ti: (bi * nt + ti, 0)),
                   pl.BlockSpec((None, CONV_HALO, cdim), lambda bi, ti: (bi, 0, 0))),
        scratch_shapes=[pltpu.VMEM((CONV_HALO + tt, cdim), F32)],
        compiler_params=_cparams(("arbitrary", "arbitrary")),
        name="conv_branch",
    )(z, z, conv_buf, cw, cb.reshape(1, cdim), lw.reshape(1, cdim), lb.reshape(1, cdim))


def _tri_cumsum(tri_ref, x):
    hi, mid, lo = _split3(x)
    tri = tri_ref[...]
    return _dot(tri, hi) + _dot(tri, mid) + _dot(tri, lo)


LOG2E = 1.4426950408889634
N_BIAS_TERMS = 3


def _logf_kernel(zf_ref, bf_ref, tri_ref, lf_ref, c_ref, ct_ref, fq_ref, fk_ref, carry_ref, *, n_heads):
    @pl.when(pl.program_id(1) == 0)
    def _():
        carry_ref[...] = jnp.zeros_like(carry_ref)

    lf = _log_sigmoid(zf_ref[...] + bf_ref[...])
    c = _tri_cumsum(tri_ref, lf) + carry_ref[...]
    carry_ref[...] = c[LANES - 1:LANES, :]
    lf_ref[...] = lf
    c_ref[...] = c
    ct_ref[...] = jnp.transpose(c)[0:SUBLANES, :]

    lane = lax.broadcasted_iota(jnp.int32, c.shape, 1)
    ones_lanes = jnp.where(lane < 2 * N_BIAS_TERMS, 1.0, 0.0)
    for h in range(n_heads):
        col = jnp.sum(jnp.where(lane == h, c, 0.0), axis=1, keepdims=True) * LOG2E
        fq = ones_lanes
        fk = ones_lanes
        for i, t in enumerate(_split3(col)):
            t32 = t.astype(F32)
            fq = jnp.where(lane == i, t32, fq)
            fk = jnp.where(lane == N_BIAS_TERMS + i, -t32, fk)
        fq_ref[:, h * LANES:(h + 1) * LANES] = fq.astype(BF16)
        fk_ref[:, h * LANES:(h + 1) * LANES] = fk.astype(BF16)


def forget_cumsum(zf, b_forget_row, tri, seq_len, n_heads):
    n = zf.shape[0]
    b = n // seq_len
    nt = seq_len // LANES
    w = n_heads * LANES
    return pl.pallas_call(
        functools.partial(_logf_kernel, n_heads=n_heads),
        out_shape=(jax.ShapeDtypeStruct((n, LANES), F32), jax.ShapeDtypeStruct((n, LANES), F32),
                   jax.ShapeDtypeStruct((b, SUBLANES, seq_len), F32),
                   jax.ShapeDtypeStruct((n, w), BF16), jax.ShapeDtypeStruct((n, w), BF16)),
        grid=(b, nt),
        in_specs=[
            pl.BlockSpec((LANES, LANES), lambda bi, ti: (bi * nt + ti, 0)),
            pl.BlockSpec((1, LANES), lambda bi, ti: (0, 0)),
            pl.BlockSpec((LANES, LANES), lambda bi, ti: (0, 0)),
        ],
        out_specs=(pl.BlockSpec((LANES, LANES), lambda bi, ti: (bi * nt + ti, 0)),
                   pl.BlockSpec((LANES, LANES), lambda bi, ti: (bi * nt + ti, 0)),
                   pl.BlockSpec((None, SUBLANES, LANES), lambda bi, ti: (bi, 0, ti)),
                   pl.BlockSpec((LANES, w), lambda bi, ti: (bi * nt + ti, 0)),
                   pl.BlockSpec((LANES, w), lambda bi, ti: (bi * nt + ti, 0))),
        scratch_shapes=[pltpu.VMEM((1, LANES), F32)],
        compiler_params=_cparams(("arbitrary", "arbitrary")),
        name="forget_cumsum",
    )(zf, b_forget_row, tri)


FOX_COL_BLOCK = 2 * HEAD_DIM


def _fox_kernel(qi_ref, ki_ref, *refs, tq, tk, n_heads):
    nb = n_heads * HEAD_DIM // FOX_COL_BLOCK
    q_refs, k_refs, v_refs = refs[0:nb], refs[nb:2 * nb], refs[2 * nb:3 * nb]
    fq_ref, fk_ref, o_ref, m_scr, l_scr, acc_scr, qa_scr = refs[3 * nb:]
    step = pl.program_id(1)
    qi = qi_ref[step]
    ki = ki_ref[step]

    def head_cols(block_refs, h):
        lo = (h * HEAD_DIM) % FOX_COL_BLOCK
        return block_refs[h * HEAD_DIM // FOX_COL_BLOCK][:, lo:lo + HEAD_DIM]

    @pl.when(ki == 0)
    def _():
        m_scr[...] = jnp.full_like(m_scr, NEG_INF)
        l_scr[...] = jnp.zeros_like(l_scr)
        acc_scr[...] = jnp.zeros_like(acc_scr)
        for h in range(n_heads):
            qa_scr[h, :, 0:HEAD_DIM] = (head_cols(q_refs, h) * (HEAD_DIM ** -0.5 * LOG2E)).astype(BF16)
            qa_scr[h, :, HEAD_DIM:] = fq_ref[:, h * LANES:(h + 1) * LANES]

    def update(h, masked):
        ka = jnp.concatenate([head_cols(k_refs, h).astype(BF16), fk_ref[:, h * LANES:(h + 1) * LANES]], axis=1)
        vb = head_cols(v_refs, h).astype(BF16)
        s = _dot_nt(qa_scr[h], ka)
        if masked:
            row = lax.broadcasted_iota(jnp.int32, (tq, tk), 0)
            col = lax.broadcasted_iota(jnp.int32, (tq, tk), 1)
            s = jnp.where(col <= row, s, NEG_INF)
        m_prev = m_scr[h]
        m_new = jnp.maximum(m_prev, jnp.max(s, axis=1, keepdims=True))
        alpha = jnp.exp2(m_prev - m_new)
        p = jnp.exp2(s - m_new[:, 0:1])
        l_scr[h] = alpha * l_scr[h] + jnp.sum(p, axis=1, keepdims=True)
        acc_scr[h] = alpha * acc_scr[h] + _dot(p.astype(BF16), vb)
        m_scr[h] = m_new

    @pl.when(ki < qi)
    def _():
        for h in range(n_heads):
            update(h, False)

    @pl.when(ki == qi)
    def _():
        for h in range(n_heads):
            update(h, True)
            o_ref[:, h * HEAD_DIM:(h + 1) * HEAD_DIM] = acc_scr[h] / l_scr[h]


def fox_attention(z, col_q, col_k, col_v, fq, fk, n_heads, seq_len, tq=512):
    n = z.shape[0]
    b = n // seq_len
    tk = tq
    nq = seq_len // tq
    hd = HEAD_DIM
    w = n_heads * hd
    nb = w // FOX_COL_BLOCK
    pairs = [(qi, ki) for qi in range(nq) for ki in range(qi + 1)]
    qi_tab = jnp.asarray([p[0] for p in pairs], jnp.int32)
    ki_tab = jnp.asarray([p[1] for p in pairs], jnp.int32)

    def cols(col0, use_q):
        j0 = col0 // FOX_COL_BLOCK
        tile = tq if use_q else tk
        if use_q:
            return [pl.BlockSpec((tile, FOX_COL_BLOCK), lambda bi, s, qt, kt, j=j0 + i: (bi * nq + qt[s], j))
                    for i in range(nb)]
        return [pl.BlockSpec((tile, FOX_COL_BLOCK), lambda bi, s, qt, kt, j=j0 + i: (bi * nq + kt[s], j))
                for i in range(nb)]

    grid_spec = pltpu.PrefetchScalarGridSpec(
        num_scalar_prefetch=2,
        grid=(b, len(pairs)),
        in_specs=cols(col_q, True) + cols(col_k, False) + cols(col_v, False) + [
            pl.BlockSpec((tq, w), lambda bi, s, qt, kt: (bi * nq + qt[s], 0)),
            pl.BlockSpec((tk, w), lambda bi, s, qt, kt: (bi * nq + kt[s], 0)),
        ],
        out_specs=pl.BlockSpec((tq, w), lambda bi, s, qt, kt: (bi * nq + qt[s], 0)),
        scratch_shapes=[pltpu.VMEM((n_heads, tq, LANES), F32), pltpu.VMEM((n_heads, tq, LANES), F32),
                        pltpu.VMEM((n_heads, tq, hd), F32), pltpu.VMEM((n_heads, tq, hd + LANES), BF16)],
    )
    return pl.pallas_call(
        functools.partial(_fox_kernel, tq=tq, tk=tk, n_heads=n_heads),
        out_shape=jax.ShapeDtypeStruct((n, w), F32),
        grid_spec=grid_spec,
        compiler_params=_cparams(("arbitrary", "arbitrary")),
        name="fox_attention",
    )(qi_tab, ki_tab, *([z] * (3 * nb)), fq, fk)


def _fox_paged_kernel(pt_ref, *refs, n_group, n_heads, t_pad, n_steps):
    del pt_ref
    g_ = n_group
    k_refs = refs[0:g_]
    v_refs = refs[g_:2 * g_]
    lf_refs = refs[2 * g_:3 * g_]
    (q_ref, kn_ref, vn_ref, cn_ref, ctn_ref, tri_ref, ones_ref, o_ref,
     m_scr, l_scr, acc_scr, carry_scr) = refs[3 * g_:]
    g = pl.program_id(1)
    scale = HEAD_DIM ** -0.5

    @pl.when(g == 0)
    def _():
        m_scr[...] = jnp.full_like(m_scr, NEG_INF)
        l_scr[...] = jnp.zeros_like(l_scr)
        acc_scr[...] = jnp.zeros_like(acc_scr)
        carry_scr[...] = jnp.zeros_like(carry_scr)

    lf_all = jnp.concatenate([r[...] for r in lf_refs], axis=0)
    hi, mid, lo = _split3(lf_all)
    tri = tri_ref[...]
    ones = ones_ref[...]
    after = _dot(hi, tri) + _dot(mid, tri) + _dot(lo, tri)
    total = _dot(hi, ones) + _dot(mid, ones) + _dot(lo, ones)
    carry = carry_scr[...]
    page_bias = []
    for i in range(g_):
        page_bias.append(carry + after[i * SUBLANES:(i + 1) * SUBLANES, :])
        carry = carry + total[i * SUBLANES:(i + 1) * SUBLANES, :]
    carry_scr[...] = carry

    cn = cn_ref[...]
    lane = lax.broadcasted_iota(jnp.int32, cn.shape, 1)

    def flash_update(h, s, v_list):
        m_prev = m_scr[h]
        m_new = jnp.maximum(m_prev, jnp.max(s, axis=1, keepdims=True))
        alpha = jnp.exp(m_prev - m_new)
        p32 = jnp.exp(s - m_new[:, 0:1])
        l_scr[h] = alpha * l_scr[h] + jnp.sum(p32, axis=1, keepdims=True)
        p = p32.astype(BF16)
        pv = _dot(p[:, 0:LANES], v_list[0])
        for i in range(1, len(v_list)):
            pv = pv + _dot(p[:, i * LANES:(i + 1) * LANES], v_list[i])
        acc_scr[h] = alpha * acc_scr[h] + pv
        m_scr[h] = m_new

    def scores(h):
        qh = q_ref[:, h * HEAD_DIM:(h + 1) * HEAD_DIM].astype(BF16)
        cq = jnp.sum(jnp.where(lane == h, cn, 0.0), axis=1, keepdims=True)
        s_list = []
        for i in range(g_):
            kh = k_refs[i][h].astype(BF16)
            s_list.append(_dot_nt(qh, kh) * scale + (cq + page_bias[i][h:h + 1, :]))
        return jnp.concatenate(s_list, axis=1)

    s_all = [scores(h) for h in range(n_heads)]
    for h in range(n_heads):
        flash_update(h, s_all[h], [v_refs[i][h].astype(BF16) for i in range(g_)])

    @pl.when(g == n_steps - 1)
    def _():
        row = lax.broadcasted_iota(jnp.int32, (t_pad, LANES), 0)
        col = lax.broadcasted_iota(jnp.int32, (t_pad, LANES), 1)
        for h in range(n_heads):
            hs = slice(h * HEAD_DIM, (h + 1) * HEAD_DIM)
            qh = q_ref[:, hs].astype(BF16)
            cq = jnp.sum(jnp.where(lane == h, cn, 0.0), axis=1, keepdims=True)
            s = _dot_nt(qh, kn_ref[:, hs].astype(BF16)) * scale + (cq - ctn_ref[h:h + 1, :])
            s = jnp.where(col <= row, s, NEG_INF)
            flash_update(h, s, [vn_ref[:, hs].astype(BF16)])
            o_ref[:, hs] = acc_scr[h] / l_scr[h]


def fox_paged_attention(q, k_new, v_new, c_new, ct_new, cache_k, cache_v, logf_t, page_table, layer,
                        tri_after, ones_mat, n_group=16, t_pad=SUBLANES):
    b, n_pages = page_table.shape
    n_heads = cache_k.shape[2]
    page = cache_k.shape[3]
    hd = HEAD_DIM
    w = n_heads * hd
    ng = n_pages // n_group

    def page_of(i):
        return lambda bi, gi, pt: (pt[bi, n_pages - 1 - (gi * n_group + i)], layer, 0, 0, 0)

    def lf_of(i):
        return lambda bi, gi, pt: (layer, pt[bi, n_pages - 1 - (gi * n_group + i)], 0, 0)

    kv_specs = [pl.BlockSpec((None, None, n_heads, page, hd), page_of(i)) for i in range(n_group)]
    lf_specs = [pl.BlockSpec((None, None, SUBLANES, page), lf_of(i)) for i in range(n_group)]
    grid_spec = pltpu.PrefetchScalarGridSpec(
        num_scalar_prefetch=1,
        grid=(b, ng),
        in_specs=kv_specs + kv_specs + lf_specs + [
            pl.BlockSpec((None, t_pad, w), lambda bi, gi, pt: (bi, 0, 0)),
            pl.BlockSpec((None, LANES, w), lambda bi, gi, pt: (bi, 0, 0)),
            pl.BlockSpec((None, LANES, w), lambda bi, gi, pt: (bi, 0, 0)),
            pl.BlockSpec((t_pad, LANES), lambda bi, gi, pt: (bi * (LANES // t_pad), 0)),
            pl.BlockSpec((None, SUBLANES, LANES), lambda bi, gi, pt: (bi, 0, 0)),
            pl.BlockSpec((LANES, LANES), lambda bi, gi, pt: (0, 0)),
            pl.BlockSpec((LANES, LANES), lambda bi, gi, pt: (0, 0)),
        ],
        out_specs=pl.BlockSpec((None, t_pad, w), lambda bi, gi, pt: (bi, 0, 0)),
        scratch_shapes=[pltpu.VMEM((n_heads, t_pad, LANES), F32), pltpu.VMEM((n_heads, t_pad, LANES), F32),
                        pltpu.VMEM((n_heads, t_pad, hd), F32), pltpu.VMEM((SUBLANES, LANES), F32)],
    )
    return pl.pallas_call(
        functools.partial(_fox_paged_kernel, n_group=n_group, n_heads=n_heads, t_pad=t_pad, n_steps=ng),
        out_shape=jax.ShapeDtypeStruct((b, t_pad, w), F32),
        grid_spec=grid_spec,
        compiler_params=_cparams(("arbitrary", "arbitrary")),
        name="fox_paged_attention",
    )(page_table, *([cache_k] * n_group), *([cache_v] * n_group), *([logf_t] * n_group),
      q, k_new, v_new, c_new, ct_new, tri_after, ones_mat)


def _router_kernel(x_ref, sc_ref, sh_ref, wr_ref, br_ref, h_ref, info_ref, *, n_experts):
    h = x_ref[...] * (1.0 + sc_ref[...]) + sh_ref[...]
    h_ref[...] = h
    logits = _dot(h.astype(BF16), wr_ref[...].astype(BF16)) + br_ref[...]
    lane = lax.broadcasted_iota(jnp.int32, logits.shape, 1)
    neg = -jnp.inf
    l1 = jnp.where(lane < n_experts, logits, neg)
    m1 = jnp.max(l1, axis=1, keepdims=True)
    i1 = jnp.min(jnp.where(l1 == m1, lane, LANES), axis=1, keepdims=True)
    l2 = jnp.where(lane == i1, neg, l1)
    m2 = jnp.max(l2, axis=1, keepdims=True)
    i2 = jnp.min(jnp.where(l2 == m2, lane, LANES), axis=1, keepdims=True)
    e = jnp.exp(m2 - m1)
    den = 1.0 + e
    p1 = 1.0 / den
    p2 = e / den
    info = jnp.where(lane == 0, i1.astype(F32),
                     jnp.where(lane == 1, i2.astype(F32),
                               jnp.where(lane == 2, p1, jnp.where(lane == 3, p2, 0.0))))
    info_ref[...] = info


def moe_route(x, sc, sh, w_router_pad, b_router_pad, n_experts, tm, rows_per_seq):
    n, d = x.shape
    return pl.pallas_call(
        functools.partial(_router_kernel, n_experts=n_experts),
        out_shape=(jax.ShapeDtypeStruct((n, d), F32), jax.ShapeDtypeStruct((n, LANES), F32)),
        grid=(n // tm,),
        in_specs=[
            pl.BlockSpec((tm, d), lambda i: (i, 0)),
            _mod_spec(sc, tm, rows_per_seq, 1),
            _mod_spec(sh, tm, rows_per_seq, 1),
            pl.BlockSpec((d, LANES), lambda i: (0, 0)),
            pl.BlockSpec((1, LANES), lambda i: (0, 0)),
        ],
        out_specs=(pl.BlockSpec((tm, d), lambda i: (i, 0)), pl.BlockSpec((tm, LANES), lambda i: (i, 0))),
        compiler_params=_cparams(("arbitrary",)),
        name="moe_router",
    )(x, sc, sh, w_router_pad, b_router_pad)


def moe_plan(experts, n_experts, tm):
    n = experts.shape[0]
    na = n * TOP_K
    n_tiles = pl.cdiv(na, tm) + n_experts
    flat_e = experts.reshape(na)
    onehot = (flat_e[:, None] == jnp.arange(n_experts, dtype=jnp.int32)[None, :]).astype(jnp.int32)
    incl = jnp.cumsum(onehot, axis=0)
    rank = jnp.sum((incl - onehot) * onehot, axis=1)
    counts = incl[-1]
    tiles_e = (counts + tm - 1) // tm
    tile_end = jnp.cumsum(tiles_e)
    tile_off = tile_end - tiles_e
    n_used = tile_end[-1:]
    dest = (jnp.sum(tile_off[None, :] * onehot, axis=1) * tm + rank).astype(jnp.int32)
    src_token = jnp.zeros((n_tiles * tm,), jnp.int32).at[dest].set(jnp.arange(na, dtype=jnp.int32) // TOP_K)
    t = jnp.arange(n_tiles, dtype=jnp.int32)
    tile_expert = jnp.sum((t[:, None] >= tile_end[None, :]).astype(jnp.int32), axis=1)
    tile_expert = jnp.minimum(tile_expert, n_experts - 1).astype(jnp.int32)
    return dest, src_token, tile_expert, n_used.astype(jnp.int32)


def _row_copy(src_ref, row, dst_ref, slot, sem):
    return pltpu.make_async_copy(src_ref.at[pl.ds(row, 1), :], dst_ref.at[pl.ds(slot, 1), :], sem)


GATHER_UNROLL = 8


def _moe_gather_kernel(src_ref, nused_ref, h_ref, o_ref, buf, sem, *, tm):
    t = pl.program_id(0)

    @pl.when(t < nused_ref[0])
    def _():
        base = t * tm

        def issue(r8, carry):
            for u in range(GATHER_UNROLL):
                r = r8 * GATHER_UNROLL + u
                _row_copy(h_ref, src_ref[base + r], buf, r, sem).start(priority=u % 2)
            return carry

        lax.fori_loop(0, tm // GATHER_UNROLL, issue, 0)

        def drain(r, carry):
            _row_copy(h_ref, 0, buf, r, sem).wait()
            return carry

        lax.fori_loop(0, tm, drain, 0, unroll=GATHER_UNROLL)
        o_ref[...] = buf[...].astype(BF16)

    @pl.when(t >= nused_ref[0])
    def _():
        o_ref[...] = jnp.zeros_like(o_ref)


def moe_gather(h, src_token, n_used, tm):
    s = src_token.shape[0]
    d = h.shape[1]
    n_tiles = s // tm
    grid_spec = pltpu.PrefetchScalarGridSpec(
        num_scalar_prefetch=2,
        grid=(n_tiles,),
        in_specs=[pl.BlockSpec(memory_space=pl.ANY)],
        out_specs=pl.BlockSpec((tm, d), lambda t, src, nu: (t, 0)),
        scratch_shapes=[pltpu.VMEM((tm, d), F32), pltpu.SemaphoreType.DMA(())],
    )
    return pl.pallas_call(
        functools.partial(_moe_gather_kernel, tm=tm),
        out_shape=jax.ShapeDtypeStruct((s, d), BF16),
        grid_spec=grid_spec,
        compiler_params=_cparams(("arbitrary",)),
        name="moe_gather",
    )(src_token, n_used, h)


def _new_expert(te_ref, nused_ref, t):
    last = nused_ref[0] - 1
    cur = jnp.minimum(t, last)
    prev = jnp.maximum(cur - 1, 0)
    return jnp.logical_and(t <= last, jnp.logical_or(t == 0, te_ref[cur] != te_ref[prev]))


def _moe_up_kernel(te_ref, nused_ref, x_ref, wg_ref, wu_ref, a_ref, wgb_scr, wub_scr):
    t = pl.program_id(1)

    @pl.when(_new_expert(te_ref, nused_ref, t))
    def _():
        wgb_scr[...] = wg_ref[...].astype(BF16)
        wub_scr[...] = wu_ref[...].astype(BF16)

    @pl.when(t < nused_ref[0])
    def _():
        xb = x_ref[...]
        gate = _dot(xb, wgb_scr[...])
        up = _dot(xb, wub_scr[...])
        a_ref[...] = (_silu(gate) * up).astype(a_ref.dtype)

    @pl.when(t >= nused_ref[0])
    def _():
        a_ref[...] = jnp.zeros_like(a_ref)


def moe_up(xs, w_gate, w_up, idx, tile_expert, n_used, tm, tf=512):
    s, d = xs.shape
    ff = w_gate.shape[3]
    n_tiles = s // tm

    def tile(t, nu):
        return jnp.minimum(t, nu[0] - 1)

    grid_spec = pltpu.PrefetchScalarGridSpec(
        num_scalar_prefetch=2,
        grid=(pl.cdiv(ff, tf), n_tiles),
        in_specs=[
            pl.BlockSpec((tm, d), lambda f, t, te, nu: (tile(t, nu), 0)),
            pl.BlockSpec((None, None, d, tf), lambda f, t, te, nu: (idx, te[tile(t, nu)], 0, f)),
            pl.BlockSpec((None, None, d, tf), lambda f, t, te, nu: (idx, te[tile(t, nu)], 0, f)),
        ],
        out_specs=pl.BlockSpec((tm, tf), lambda f, t, te, nu: (t, f)),
        scratch_shapes=[pltpu.VMEM((d, tf), BF16), pltpu.VMEM((d, tf), BF16)],
    )
    return pl.pallas_call(
        _moe_up_kernel,
        out_shape=jax.ShapeDtypeStruct((s, ff), BF16),
        grid_spec=grid_spec,
        compiler_params=_cparams(("arbitrary", "arbitrary")),
        name="moe_up",
    )(tile_expert, n_used, xs, w_gate, w_up)


def _moe_down_kernel(te_ref, nused_ref, a_ref, w_ref, o_ref, wb_scr):
    t = pl.program_id(1)

    @pl.when(_new_expert(te_ref, nused_ref, t))
    def _():
        wb_scr[...] = w_ref[...].astype(BF16)

    @pl.when(t < nused_ref[0])
    def _():
        o_ref[...] = _dot(a_ref[...], wb_scr[...])

    @pl.when(t >= nused_ref[0])
    def _():
        o_ref[...] = jnp.zeros_like(o_ref)


def moe_down(a, w_down, idx, tile_expert, n_used, tm, tn=512):
    s, ff = a.shape
    d = w_down.shape[3]
    n_tiles = s // tm

    def tile(t, nu):
        return jnp.minimum(t, nu[0] - 1)

    grid_spec = pltpu.PrefetchScalarGridSpec(
        num_scalar_prefetch=2,
        grid=(d // tn, n_tiles),
        in_specs=[
            pl.BlockSpec((tm, ff), lambda j, t, te, nu: (tile(t, nu), 0)),
            pl.BlockSpec((None, None, ff, tn), lambda j, t, te, nu: (idx, te[tile(t, nu)], 0, j)),
        ],
        out_specs=pl.BlockSpec((tm, tn), lambda j, t, te, nu: (t, j)),
        scratch_shapes=[pltpu.VMEM((ff, tn), BF16)],
    )
    return pl.pallas_call(
        _moe_down_kernel,
        out_shape=jax.ShapeDtypeStruct((s, d), F32),
        grid_spec=grid_spec,
        compiler_params=_cparams(("arbitrary", "arbitrary")),
        name="moe_down",
    )(tile_expert, n_used, a, w_down)


def _moe_combine_kernel(dest_ref, y_ref, info_ref, x_ref, g_ref, w_ref, b_ref, o_ref, buf0, buf1, sem,
                        *, tm, row0, alpha):
    base = (row0 + pl.program_id(0) * tm) * TOP_K

    def issue(r8, carry):
        for u in range(GATHER_UNROLL):
            r = r8 * GATHER_UNROLL + u
            _row_copy(y_ref, dest_ref[base + TOP_K * r], buf0, r, sem).start(priority=0)
            _row_copy(y_ref, dest_ref[base + TOP_K * r + 1], buf1, r, sem).start(priority=1)
        return carry

    lax.fori_loop(0, tm // GATHER_UNROLL, issue, 0)

    def drain(r, carry):
        _row_copy(y_ref, 0, buf0, r, sem).wait()
        _row_copy(y_ref, 0, buf1, r, sem).wait()
        return carry

    lax.fori_loop(0, tm, drain, 0, unroll=GATHER_UNROLL)
    info = info_ref[...]
    f = info[:, 2:3] * buf0[...] + info[:, 3:4] * buf1[...]
    y = alpha * x_ref[...] + g_ref[...] * f
    o_ref[...] = _layer_norm_rows(y, w_ref[...], b_ref[...])


def moe_combine_ln(y_sorted, dest, info, x, g, w, b, row0, alpha, tm, rows_per_seq):
    n, d = x.shape
    gspec = _mod_spec(g, tm, rows_per_seq, 1)
    gmap = gspec.index_map
    grid_spec = pltpu.PrefetchScalarGridSpec(
        num_scalar_prefetch=1,
        grid=(n // tm,),
        in_specs=[
            pl.BlockSpec(memory_space=pl.ANY),
            pl.BlockSpec((tm, LANES), lambda i, dst: (i, 0)),
            pl.BlockSpec((tm, d), lambda i, dst: (i, 0)),
            pl.BlockSpec(gspec.block_shape, lambda i, dst: gmap(i)),
            pl.BlockSpec((1, d), lambda i, dst: (0, 0)),
            pl.BlockSpec((1, d), lambda i, dst: (0, 0)),
        ],
        out_specs=pl.BlockSpec((tm, d), lambda i, dst: (i, 0)),
        scratch_shapes=[pltpu.VMEM((tm, d), F32), pltpu.VMEM((tm, d), F32), pltpu.SemaphoreType.DMA(())],
    )
    return pl.pallas_call(
        functools.partial(_moe_combine_kernel, tm=tm, row0=row0, alpha=alpha),
        out_shape=jax.ShapeDtypeStruct((n, d), F32),
        grid_spec=grid_spec,
        compiler_params=_cparams(("arbitrary",)),
        name="moe_combine_ln",
    )(dest, y_sorted, info, x, g, w.reshape(1, d), b.reshape(1, d))


def _kv_pack_kernel(*refs, depth, n_heads):
    nb = n_heads * HEAD_DIM // FOX_COL_BLOCK
    k_out, v_out = refs[2 * depth * nb:]
    for l in range(depth):
        for which, out in ((0, k_out), (1, v_out)):
            blocks = refs[(2 * l + which) * nb:(2 * l + which + 1) * nb]
            for h in range(n_heads):
                lo = (h * HEAD_DIM) % FOX_COL_BLOCK
                out[l, h] = blocks[h * HEAD_DIM // FOX_COL_BLOCK][:, lo:lo + HEAD_DIM]


def kv_pack(z_layers, col_k, col_v, n_heads, seq_len, tt=256):
    depth = len(z_layers)
    n = z_layers[0].shape[0]
    b = n // seq_len
    nt = seq_len // tt
    nb = n_heads * HEAD_DIM // FOX_COL_BLOCK

    def cols(col0):
        j0 = col0 // FOX_COL_BLOCK
        return [pl.BlockSpec((tt, FOX_COL_BLOCK), lambda bi, ti, j=j0 + i: (bi * nt + ti, j)) for i in range(nb)]

    in_specs, operands = [], []
    for z in z_layers:
        in_specs += cols(col_k) + cols(col_v)
        operands += [z] * (2 * nb)
    out_sds = jax.ShapeDtypeStruct((b, depth, n_heads, seq_len, HEAD_DIM), F32)
    out_spec = pl.BlockSpec((None, depth, n_heads, tt, HEAD_DIM), lambda bi, ti: (bi, 0, 0, ti, 0))
    return pl.pallas_call(
        functools.partial(_kv_pack_kernel, depth=depth, n_heads=n_heads),
        out_shape=(out_sds, out_sds),
        grid=(b, nt),
        in_specs=in_specs,
        out_specs=(out_spec, out_spec),
        compiler_params=_cparams(("arbitrary", "arbitrary")),
        name="kv_pack",
    )(*operands)


PROMPT_TM = 1024
PROMPT_LN_TM = 512
MOE_TM = 512
MOE_COMBINE_TM = 256


def _pad_axis(a, axis, size):
    pad = [(0, 0)] * a.ndim
    pad[axis] = (0, size - a.shape[axis])
    return jnp.pad(a, pad)


def kernel(x_prompt, x_sample, cache_k, cache_v, cache_logf, state_ret, state_conv, page_table, c_prompt, c_sample, w_ada, b_ada, w_in, b_forget, ret_gn_w, conv_w, conv_b, conv_ln_w, conv_ln_b, w_out, ln1_w, ln1_b, ln2_w, ln2_b, ffn_w_gate, ffn_w_up, ffn_w_down, moe_router, moe_router_b, moe_w_gate, moe_w_up, moe_w_down):
    bp, tp, d = x_prompt.shape
    bs, ts, _ = x_sample.shape
    depth = w_in.shape[0]
    ret_heads = state_ret.shape[2]
    fox_heads = cache_k.shape[3]
    page = cache_k.shape[2]
    conv_dim = state_conv.shape[3]
    n_experts = moe_router.shape[2]
    hd = HEAD_DIM
    ret_w, fox_w = ret_heads * hd, fox_heads * hd
    o_ca = 4 * ret_w
    o_cb = o_ca + conv_dim
    o_fq = o_cb + conv_dim
    o_fk = o_fq + fox_w
    o_fv = o_fk + fox_w
    n_main = o_fv + fox_w
    alpha = (2 * depth) ** 0.25
    past_len = page_table.shape[1] * page
    tpad = RET_CHUNK
    np_rows, ns_rows = bp * tp, bs * ts

    w_forget = _pad_axis(w_in[:, :, n_main:], 2, LANES)
    bf_rows = _pad_axis(b_forget, 1, LANES).reshape(depth, 1, LANES)
    router_w = _pad_axis(moe_router, 2, LANES)
    router_b = _pad_axis(moe_router_b, 1, LANES).reshape(-1, 1, LANES)
    conv_w_pad = _pad_axis(conv_w, 1, CONV_HALO)
    logf_t = _pad_axis(jnp.transpose(cache_logf, (1, 0, 3, 2)), 2, SUBLANES)
    cache_k_hm = jnp.transpose(cache_k, (0, 1, 3, 2, 4))
    cache_v_hm = jnp.transpose(cache_v, (0, 1, 3, 2, 4))
    tri_incl = jnp.tril(jnp.ones((LANES, LANES), BF16))
    tri_after = jnp.tril(jnp.ones((page, page), BF16), -1)
    ones_mat = jnp.ones((page, page), BF16)
    tabs_p = retention_tables(ret_heads, tp, 0, tp)
    tabs_s = retention_tables(ret_heads, ts, past_len, tpad)
    r0_p = jnp.zeros((bp, ret_heads, hd, hd), F32)
    buf0_p = jnp.zeros((bp, CONV_HALO, conv_dim), F32)

    c_rows = _pad_axis(jnp.concatenate([c_prompt, c_sample], axis=0), 0, 2 * SUBLANES)
    mod = ada_modulation(c_rows, w_ada, b_ada)

    xp = x_prompt.reshape(np_rows, d)
    xs = x_sample.reshape(ns_rows, d)
    rets_p, rets_s, convs_p, convs_s, zs_p, ks_s, vs_s, lfs_p, lfs_s = ([] for _ in range(9))

    def unpad(a):
        return a.reshape(bs, tpad, -1)[:, :ts].reshape(ns_rows, -1)

    for l in range(depth):
        parts = [mod[l, :, i * d:(i + 1) * d] for i in range(6)]
        sh1p, sc1p, g1p, sh2p, sc2p, g2p = (p[:bp].reshape(bp, 1, d) for p in parts)
        sh1s, sc1s, g1s, sh2s, sc2s, g2s = (jnp.repeat(p[bp:bp + bs], ts, axis=0) for p in parts)

        z, zf = proj_in(xp, sc1p, sh1p, w_in, w_forget, l, n_main, PROMPT_TM, tp, tn=640)
        ret_o, r_new = retention(z, tabs_p, ret_gn_w[l], r0_p, ret_heads, tp)
        conv_o, conv_new = conv_branch(z, o_ca, o_cb, buf0_p, conv_w_pad[l], conv_b[l], conv_ln_w[l],
                                       conv_ln_b[l], tp, RET_CHUNK)
        lf, _, _, bias_q, bias_k = forget_cumsum(zf, bf_rows[l], tri_incl, tp, fox_heads)
        fox_o = fox_attention(z, o_fq, o_fk, o_fv, bias_q, bias_k, fox_heads, tp)
        m = proj_out(ret_o, conv_o, fox_o, w_out, l, PROMPT_TM)
        x1p = ln_residual(xp, m, g1p, ln1_w[l], ln1_b[l], alpha, PROMPT_LN_TM, tp)
        rets_p.append(r_new)
        convs_p.append(conv_new[:, CONV_HALO - (CONV_W - 1):])
        zs_p.append(z)
        lfs_p.append(lf[:, :fox_heads].reshape(bp, tp, fox_heads))

        zs, zfs = proj_in(xs, sc1s, sh1s, w_in, w_forget, l, n_main, ns_rows, ts, tn=640)
        zs_pad = _pad_axis(zs.reshape(bs, ts, n_main), 1, tpad)
        z2 = zs_pad.reshape(bs * tpad, n_main)
        zf2 = _pad_axis(zfs.reshape(bs, ts, LANES), 1, tpad).reshape(bs * tpad, LANES)
        ret_s, r_new_s = retention(z2, tabs_s, ret_gn_w[l], state_ret[:, l], ret_heads, tpad)
        buf_s = jnp.pad(state_conv[:, l], ((0, 0), (CONV_HALO - (CONV_W - 1), 0), (0, 0)))
        conv_s, conv_new_s = conv_branch(z2, o_ca, o_cb, buf_s, conv_w_pad[l], conv_b[l], conv_ln_w[l],
                                         conv_ln_b[l], tpad, ts)
        lf_s, csum_s, csum_t_s, _, _ = forget_cumsum(zf2, bf_rows[l], tri_incl, tpad, fox_heads)
        fox_s = fox_paged_attention(zs_pad[:, :SUBLANES, o_fq:o_fk], zs_pad[:, :, o_fk:o_fv],
                                    zs_pad[:, :, o_fv:n_main], csum_s, csum_t_s, cache_k_hm, cache_v_hm, logf_t,
                                    page_table, l, tri_after, ones_mat)
        fox_s = fox_s[:, :ts].reshape(ns_rows, fox_w)
        m_s = proj_out(unpad(ret_s), unpad(conv_s), fox_s, w_out, l, ns_rows)
        x1s = ln_residual(xs, m_s, g1s, ln1_w[l], ln1_b[l], alpha, ns_rows, ts)
        rets_s.append(r_new_s)
        convs_s.append(conv_new_s[:, CONV_HALO - (CONV_W - 1):])
        ks_s.append(zs[:, o_fk:o_fv].reshape(bs, ts, fox_heads, hd))
        vs_s.append(zs[:, o_fv:n_main].reshape(bs, ts, fox_heads, hd))
        lfs_s.append(unpad(lf_s)[:, :fox_heads].reshape(bs, ts, fox_heads))

        i = l // 2
        if l % 2 == 0:
            a = ffn_up(x1p, sc2p, sh2p, ffn_w_gate, ffn_w_up, i, PROMPT_TM, tp)
            y = ffn_down(a, ffn_w_down, i, PROMPT_TM)
            xp = ln_residual(x1p, y, g2p, ln2_w[l], ln2_b[l], alpha, PROMPT_LN_TM, tp)
            a_s = ffn_up(x1s, sc2s, sh2s, ffn_w_gate, ffn_w_up, i, ns_rows, ts)
            y_s = ffn_down(a_s, ffn_w_down, i, ns_rows)
            xs = ln_residual(x1s, y_s, g2s, ln2_w[l], ln2_b[l], alpha, ns_rows, ts)
        else:
            h_p, info_p = moe_route(x1p, sc2p, sh2p, router_w[i], router_b[i], n_experts, PROMPT_LN_TM, tp)
            h_s, info_s = moe_route(x1s, sc2s, sh2s, router_w[i], router_b[i], n_experts, ns_rows, ts)
            experts = jnp.concatenate([info_p[:, :TOP_K], info_s[:, :TOP_K]], axis=0).astype(jnp.int32)
            dest, src_token, tile_expert, n_used = moe_plan(experts, n_experts, MOE_TM)
            x_sorted = moe_gather(jnp.concatenate([h_p, h_s], axis=0), src_token, n_used, MOE_TM)
            a = moe_up(x_sorted, moe_w_gate, moe_w_up, i, tile_expert, n_used, MOE_TM)
            y = moe_down(a, moe_w_down, i, tile_expert, n_used, MOE_TM)
            xp = moe_combine_ln(y, dest, info_p, x1p, g2p, ln2_w[l], ln2_b[l], 0, alpha, MOE_COMBINE_TM, tp)
            xs = moe_combine_ln(y, dest, info_s, x1s, g2s, ln2_w[l], ln2_b[l], np_rows, alpha, ns_rows, ts)

    st = lambda lst: jnp.stack(lst, axis=1)
    k_hm, v_hm = kv_pack(zs_p, o_fk, o_fv, fox_heads, tp)
    k_p = jnp.transpose(k_hm, (0, 1, 3, 2, 4))
    v_p = jnp.transpose(v_hm, (0, 1, 3, 2, 4))
    return (xp.reshape(bp, tp, d), xs.reshape(bs, ts, d), st(rets_p), st(rets_s), st(convs_p), st(convs_s),
            k_p, st(ks_s), v_p, st(vs_s), st(lfs_p), st(lfs_s))
```

```python
import functools

import jax
import jax.numpy as jnp
from jax import lax
from jax.experimental import pallas as pl
from jax.experimental.pallas import tpu as pltpu

F32 = jnp.float32
BF16 = jnp.bfloat16

LANES = 128
SUBLANES = 8
VMEM_LIMIT_BYTES = 56 * 1024 * 1024

HEAD_DIM = 128
CONV_W = 31
CONV_HALO = 32
RET_CHUNK = 128
TOP_K = 2
LN_EPS = 1e-5
NEG_INF = -1e30
ROPE_BASE = 10000.0


def _cparams(sem, vmem=VMEM_LIMIT_BYTES):
    return pltpu.CompilerParams(dimension_semantics=sem, vmem_limit_bytes=vmem)


def _silu(x):
    return x * jax.nn.sigmoid(x)


def _dot(a, b):
    return jnp.dot(a, b, preferred_element_type=F32)


def _dot_nt(a, b):
    return lax.dot_general(a, b, (((1,), (1,)), ((), ())), preferred_element_type=F32)


def _split3(x):
    hi = x.astype(BF16)
    r1 = x - hi.astype(F32)
    mid = r1.astype(BF16)
    lo = (r1 - mid.astype(F32)).astype(BF16)
    return hi, mid, lo


def _log_sigmoid(x):
    return -(jnp.maximum(-x, 0.0) + jnp.log1p(jnp.exp(-jnp.abs(x))))


def _layer_norm_rows(y, w, b):
    mu = jnp.mean(y, axis=-1, keepdims=True)
    d = y - mu
    var = jnp.mean(d * d, axis=-1, keepdims=True)
    return d * lax.rsqrt(var + LN_EPS) * w + b


def _ada_kernel(c_ref, w_ref, b_ref, o_ref):
    cs = _silu(c_ref[...]).astype(BF16)
    o_ref[...] = _dot(cs, w_ref[...].astype(BF16)) + b_ref[...]


def ada_modulation(c_rows, w_ada, b_ada, tn=1024):
    depth, d, n6 = w_ada.shape
    r = c_rows.shape[0]
    return pl.pallas_call(
        _ada_kernel,
        out_shape=jax.ShapeDtypeStruct((depth, r, n6), F32),
        grid=(depth, n6 // tn),
        in_specs=[
            pl.BlockSpec((r, d), lambda l, j: (0, 0)),
            pl.BlockSpec((None, d, tn), lambda l, j: (l, 0, j)),
            pl.BlockSpec((None, 1, tn), lambda l, j: (l, 0, j)),
        ],
        out_specs=pl.BlockSpec((None, r, tn), lambda l, j: (l, 0, j)),
        compiler_params=_cparams(("arbitrary", "arbitrary")),
        name="ada_modulation",
    )(c_rows, w_ada, b_ada.reshape(depth, 1, n6))


def _mod_spec(arr, tm, rows_per_seq, grid_rank):
    d = arr.shape[-1]
    if arr.ndim == 3:
        assert rows_per_seq % tm == 0
        per = rows_per_seq // tm
        if grid_rank == 1:
            return pl.BlockSpec((None, 1, d), lambda i: (i // per, 0, 0))
        return pl.BlockSpec((None, 1, d), lambda i, j: (i // per, 0, 0))
    if grid_rank == 1:
        return pl.BlockSpec((tm, d), lambda i: (i, 0))
    return pl.BlockSpec((tm, d), lambda i, j: (i, 0))


def _proj_in_kernel(x_ref, sc_ref, sh_ref, w_ref, wf_ref, z_ref, zf_ref, xb_ref):
    @pl.when(pl.program_id(1) == 0)
    def _():
        h = x_ref[...] * (1.0 + sc_ref[...]) + sh_ref[...]
        xb_ref[...] = h.astype(BF16)
        zf_ref[...] = _dot(xb_ref[...], wf_ref[...].astype(BF16))

    z_ref[...] = _dot(xb_ref[...], w_ref[...].astype(BF16))


def proj_in(x, sc, sh, w_in, w_forget, layer, n_main, tm, rows_per_seq, tn=512):
    n, d = x.shape
    return pl.pallas_call(
        _proj_in_kernel,
        out_shape=(jax.ShapeDtypeStruct((n, n_main), F32), jax.ShapeDtypeStruct((n, LANES), F32)),
        grid=(n // tm, n_main // tn),
        in_specs=[
            pl.BlockSpec((tm, d), lambda i, j: (i, 0)),
            _mod_spec(sc, tm, rows_per_seq, 2),
            _mod_spec(sh, tm, rows_per_seq, 2),
            pl.BlockSpec((None, d, tn), lambda i, j: (layer, 0, j)),
            pl.BlockSpec((None, d, LANES), lambda i, j: (layer, 0, 0)),
        ],
        out_specs=(pl.BlockSpec((tm, tn), lambda i, j: (i, j)),
                   pl.BlockSpec((tm, LANES), lambda i, j: (i, 0))),
        scratch_shapes=[pltpu.VMEM((tm, d), BF16)],
        compiler_params=_cparams(("arbitrary", "arbitrary")),
        name="proj_in",
    )(x, sc, sh, w_in, w_forget)


def _proj_out_kernel(a_ref, b_ref, c_ref, w_ref, o_ref, xb_ref):
    @pl.when(pl.program_id(1) == 0)
    def _():
        wa, wb = a_ref.shape[1], b_ref.shape[1]
        xb_ref[:, 0:wa] = a_ref[...].astype(BF16)
        xb_ref[:, wa:wa + wb] = b_ref[...].astype(BF16)
        xb_ref[:, wa + wb:] = c_ref[...].astype(BF16)

    o_ref[...] = _dot(xb_ref[...], w_ref[...].astype(BF16))


def proj_out(a, b, c, w_out, layer, tm, tn=512):
    n = a.shape[0]
    d = w_out.shape[1]
    dout = w_out.shape[2]
    return pl.pallas_call(
        _proj_out_kernel,
        out_shape=jax.ShapeDtypeStruct((n, dout), F32),
        grid=(n // tm, dout // tn),
        in_specs=[
            pl.BlockSpec((tm, a.shape[1]), lambda i, j: (i, 0)),
            pl.BlockSpec((tm, b.shape[1]), lambda i, j: (i, 0)),
            pl.BlockSpec((tm, c.shape[1]), lambda i, j: (i, 0)),
            pl.BlockSpec((None, d, tn), lambda i, j: (layer, 0, j)),
        ],
        out_specs=pl.BlockSpec((tm, tn), lambda i, j: (i, j)),
        scratch_shapes=[pltpu.VMEM((tm, d), BF16)],
        compiler_params=_cparams(("arbitrary", "arbitrary")),
        name="proj_out",
    )(a, b, c, w_out)


def _ln_res_kernel(x_ref, m_ref, g_ref, w_ref, b_ref, *rest, alpha):
    y = alpha * x_ref[...] + g_ref[...] * m_ref[...]
    x_new = _layer_norm_rows(y, w_ref[...], b_ref[...])
    if len(rest) == 1:
        rest[0][...] = x_new
    else:
        sc_ref, sh_ref, o_ref, h_ref = rest
        o_ref[...] = x_new
        h_ref[...] = (x_new * (1.0 + sc_ref[...]) + sh_ref[...]).astype(h_ref.dtype)


def ln_residual(x, m, g, w, b, alpha, tm, rows_per_seq, next_mod=None):
    n, d = x.shape
    row = pl.BlockSpec((tm, d), lambda i: (i, 0))
    vec = pl.BlockSpec((1, d), lambda i: (0, 0))
    in_specs = [row, row, _mod_spec(g, tm, rows_per_seq, 1), vec, vec]
    operands = [x, m, g, w.reshape(1, d), b.reshape(1, d)]
    out_shape = jax.ShapeDtypeStruct((n, d), F32)
    out_specs = row
    if next_mod is not None:
        in_specs += [_mod_spec(a, tm, rows_per_seq, 1) for a in next_mod]
        operands += list(next_mod)
        out_shape = (out_shape, jax.ShapeDtypeStruct((n, d), BF16))
        out_specs = (row, row)
    return pl.pallas_call(
        functools.partial(_ln_res_kernel, alpha=alpha),
        out_shape=out_shape,
        grid=(n // tm,),
        in_specs=in_specs,
        out_specs=out_specs,
        compiler_params=_cparams(("arbitrary",)),
        name="ln_residual",
    )(*operands)


def _modulate_kernel(x_ref, sc_ref, sh_ref, h_ref):
    h_ref[...] = (x_ref[...] * (1.0 + sc_ref[...]) + sh_ref[...]).astype(h_ref.dtype)


def modulate_cast(x, sc, sh, tm, rows_per_seq):
    n, d = x.shape
    row = pl.BlockSpec((tm, d), lambda i: (i, 0))
    return pl.pallas_call(
        _modulate_kernel,
        out_shape=jax.ShapeDtypeStruct((n, d), BF16),
        grid=(n // tm,),
        in_specs=[row, _mod_spec(sc, tm, rows_per_seq, 1), _mod_spec(sh, tm, rows_per_seq, 1)],
        out_specs=row,
        compiler_params=_cparams(("arbitrary",)),
        name="modulate_cast",
    )(x, sc, sh)


def _mm_ws_kernel(*refs, n_parts):
    x_refs = refs[:n_parts]
    w_ref, o_ref, wb_scr = refs[n_parts:]

    @pl.when(pl.program_id(1) == 0)
    def _():
        wb_scr[...] = w_ref[...].astype(BF16)

    k0 = 0
    acc = None
    for x_ref in x_refs:
        kw = x_ref.shape[1]
        part = _dot(x_ref[...], wb_scr[k0:k0 + kw, :])
        acc = part if acc is None else acc + part
        k0 += kw
    o_ref[...] = acc


def matmul_ws(x_parts, w, lead, n_cols, tn, tm, name):
    n = x_parts[0].shape[0]
    k = w.shape[-2]
    lead = tuple(lead)
    in_specs = [pl.BlockSpec((tm, xp.shape[1]), lambda j, i: (i, 0)) for xp in x_parts]
    in_specs.append(pl.BlockSpec((None,) * len(lead) + (k, tn), lambda j, i: lead + (0, j)))
    return pl.pallas_call(
        functools.partial(_mm_ws_kernel, n_parts=len(x_parts)),
        out_shape=jax.ShapeDtypeStruct((n, n_cols), F32),
        grid=(n_cols // tn, n // tm),
        in_specs=in_specs,
        out_specs=pl.BlockSpec((tm, tn), lambda j, i: (i, j)),
        scratch_shapes=[pltpu.VMEM((k, tn), BF16)],
        compiler_params=_cparams(("arbitrary", "arbitrary")),
        name=name,
    )(*x_parts, w)


def _ffn_up_ws_kernel(x_ref, wg_ref, wu_ref, a_ref, wgb_scr, wub_scr):
    @pl.when(pl.program_id(1) == 0)
    def _():
        wgb_scr[...] = wg_ref[...].astype(BF16)
        wub_scr[...] = wu_ref[...].astype(BF16)

    xb = x_ref[...]
    a_ref[...] = (_silu(_dot(xb, wgb_scr[...])) * _dot(xb, wub_scr[...])).astype(a_ref.dtype)


def ffn_up_ws(xb, w_gate, w_up, idx, tm, tf=512):
    n, d = xb.shape
    ff = w_gate.shape[2]
    wspec = pl.BlockSpec((None, d, tf), lambda j, i: (idx, 0, j))
    return pl.pallas_call(
        _ffn_up_ws_kernel,
        out_shape=jax.ShapeDtypeStruct((n, ff), BF16),
        grid=(pl.cdiv(ff, tf), n // tm),
        in_specs=[pl.BlockSpec((tm, d), lambda j, i: (i, 0)), wspec, wspec],
        out_specs=pl.BlockSpec((tm, tf), lambda j, i: (i, j)),
        scratch_shapes=[pltpu.VMEM((d, tf), BF16), pltpu.VMEM((d, tf), BF16)],
        compiler_params=_cparams(("arbitrary", "arbitrary")),
        name="ffn_up_ws",
    )(xb, w_gate, w_up)


def _ffn_up_kernel(x_ref, sc_ref, sh_ref, wg_ref, wu_ref, a_ref, xb_ref):
    @pl.when(pl.program_id(1) == 0)
    def _():
        h = x_ref[...] * (1.0 + sc_ref[...]) + sh_ref[...]
        xb_ref[...] = h.astype(BF16)

    xb = xb_ref[...]
    gate = _dot(xb, wg_ref[...].astype(BF16))
    up = _dot(xb, wu_ref[...].astype(BF16))
    a_ref[...] = (_silu(gate) * up).astype(a_ref.dtype)


def ffn_up(x, sc, sh, w_gate, w_up, idx, tm, rows_per_seq, tf=512):
    n, d = x.shape
    ff = w_gate.shape[2]
    return pl.pallas_call(
        _ffn_up_kernel,
        out_shape=jax.ShapeDtypeStruct((n, ff), BF16),
        grid=(n // tm, pl.cdiv(ff, tf)),
        in_specs=[
            pl.BlockSpec((tm, d), lambda i, j: (i, 0)),
            _mod_spec(sc, tm, rows_per_seq, 2),
            _mod_spec(sh, tm, rows_per_seq, 2),
            pl.BlockSpec((None, d, tf), lambda i, j: (idx, 0, j)),
            pl.BlockSpec((None, d, tf), lambda i, j: (idx, 0, j)),
        ],
        out_specs=pl.BlockSpec((tm, tf), lambda i, j: (i, j)),
        scratch_shapes=[pltpu.VMEM((tm, d), BF16)],
        compiler_params=_cparams(("arbitrary", "arbitrary")),
        name="ffn_up",
    )(x, sc, sh, w_gate, w_up)


def _ffn_down_kernel(a_ref, w_ref, o_ref):
    o_ref[...] = _dot(a_ref[...], w_ref[...].astype(BF16))


def ffn_down(a, w_down, idx, tm, tn=256):
    n, ff = a.shape
    d = w_down.shape[2]
    return pl.pallas_call(
        _ffn_down_kernel,
        out_shape=jax.ShapeDtypeStruct((n, d), F32),
        grid=(n // tm, d // tn),
        in_specs=[
            pl.BlockSpec((tm, ff), lambda i, j: (i, 0)),
            pl.BlockSpec((None, ff, tn), lambda i, j: (idx, 0, j)),
        ],
        out_specs=pl.BlockSpec((tm, tn), lambda i, j: (i, j)),
        compiler_params=_cparams(("arbitrary", "arbitrary")),
        name="ffn_down",
    )(a, w_down)


def _retention_kernel(q_ref, k_ref, v_ref, g_ref, cos_ref, sin_ref, dmat_ref, xi_ref, zeta_ref,
                      gc_ref, gnw_ref, r0_ref, y_ref, rn_ref, r_scr, *, n_chunks, n_heads):
    c = pl.program_id(1)

    @pl.when(c == 0)
    def _():
        r_scr[...] = r0_ref[...]

    cos = cos_ref[...]
    sin = sin_ref[...]

    def rope(x):
        return x * cos + pltpu.roll(x, HEAD_DIM // 2, axis=1) * sin

    for h in range(n_heads):
        hs = slice(h * HEAD_DIM, (h + 1) * HEAD_DIM)
        q = rope(q_ref[:, hs])
        k = rope(k_ref[:, hs]) * (HEAD_DIM ** -0.5)
        qb = q.astype(BF16)
        kb = k.astype(BF16)
        vb = v_ref[:, hs].astype(BF16)
        r_prev = r_scr[h]

        s = _dot_nt(qb, kb) * dmat_ref[h]
        y = _dot(s.astype(BF16), vb) + _dot(qb, r_prev.astype(BF16)) * xi_ref[h]
        kz_t = jnp.transpose(k * zeta_ref[h]).astype(BF16)
        r_scr[h] = gc_ref[h] * r_prev + _dot(kz_t, vb)

        mu = jnp.mean(y, axis=-1, keepdims=True)
        d = y - mu
        var = jnp.mean(d * d, axis=-1, keepdims=True)
        yn = d * lax.rsqrt(var + LN_EPS) * gnw_ref[:, hs]
        y_ref[:, hs] = (yn * _silu(g_ref[:, hs])).astype(y_ref.dtype)

    @pl.when(c == n_chunks - 1)
    def _():
        rn_ref[...] = r_scr[...]


def retention(z, tabs, gn_w, r0, n_heads, seq_len):
    n = z.shape[0]
    b = n // seq_len
    cch = RET_CHUNK
    nc = seq_len // cch
    hd = HEAD_DIM
    h_ = n_heads
    w = h_ * hd
    const3 = lambda bi, ci: (0, 0, 0)
    return pl.pallas_call(
        functools.partial(_retention_kernel, n_chunks=nc, n_heads=h_),
        out_shape=(jax.ShapeDtypeStruct((n, w), BF16), jax.ShapeDtypeStruct((b, h_, hd, hd), F32)),
        grid=(b, nc),
        in_specs=[
            pl.BlockSpec((cch, w), lambda bi, ci: (bi * nc + ci, 0)),
            pl.BlockSpec((cch, w), lambda bi, ci: (bi * nc + ci, 1)),
            pl.BlockSpec((cch, w), lambda bi, ci: (bi * nc + ci, 2)),
            pl.BlockSpec((cch, w), lambda bi, ci: (bi * nc + ci, 3)),
            pl.BlockSpec((cch, hd), lambda bi, ci: (ci, 0)),
            pl.BlockSpec((cch, hd), lambda bi, ci: (ci, 0)),
            pl.BlockSpec((h_, cch, cch), const3),
            pl.BlockSpec((h_, cch, hd), const3),
            pl.BlockSpec((h_, cch, hd), const3),
            pl.BlockSpec((h_, 1, hd), const3),
            pl.BlockSpec((1, w), lambda bi, ci: (0, 0)),
            pl.BlockSpec((None, h_, hd, hd), lambda bi, ci: (bi, 0, 0, 0)),
        ],
        out_specs=(pl.BlockSpec((cch, w), lambda bi, ci: (bi * nc + ci, 0)),
                   pl.BlockSpec((None, h_, hd, hd), lambda bi, ci: (bi, 0, 0, 0))),
        scratch_shapes=[pltpu.VMEM((h_, hd, hd), F32)],
        compiler_params=_cparams(("arbitrary", "arbitrary")),
        name="retention",
    )(z, z, z, z, tabs["cos"], tabs["sin"], tabs["dmat"], tabs["xi"], tabs["zeta"], tabs["gc"],
      gn_w.reshape(1, w), r0)


def retention_tables(n_heads, t_valid, pos0, t_pad):
    cch = min(t_valid, RET_CHUNK)
    half = HEAD_DIM // 2
    lg = jnp.log(1.0 - 2.0 ** (-5.0 - jnp.arange(n_heads, dtype=F32)))
    idx = jnp.arange(cch, dtype=F32)
    diff = idx[:, None] - idx[None, :]
    dmat = jnp.where(diff[None] >= 0, jnp.exp(jnp.maximum(diff, 0.0)[None] * lg[:, None, None]), 0.0)
    xi = jnp.exp((idx[None, :] + 1.0) * lg[:, None])
    zeta = jnp.exp((cch - 1.0 - idx)[None, :] * lg[:, None])
    gc = jnp.exp(cch * lg)
    pc = RET_CHUNK - cch
    dmat = jnp.pad(dmat, ((0, 0), (0, pc), (0, pc)))
    xi = jnp.broadcast_to(jnp.pad(xi, ((0, 0), (0, pc)))[:, :, None], (n_heads, RET_CHUNK, HEAD_DIM))
    zeta = jnp.broadcast_to(jnp.pad(zeta, ((0, 0), (0, pc)))[:, :, None], (n_heads, RET_CHUNK, HEAD_DIM))
    gc = jnp.broadcast_to(gc[:, None, None], (n_heads, 1, HEAD_DIM))
    pos = (pos0 + jnp.arange(t_pad)).astype(F32)
    inv = ROPE_BASE ** (-jnp.arange(half, dtype=F32) / half)
    ang = pos[:, None] * inv[None, :]
    cos = jnp.cos(ang)
    sin = jnp.sin(ang)
    return dict(dmat=dmat, xi=xi, zeta=zeta, gc=gc,
                cos=jnp.concatenate([cos, cos], axis=1), sin=jnp.concatenate([-sin, sin], axis=1))


CONV_ROWS = 32


def _conv_kernel(za_ref, zb_ref, buf_ref, cw_ref, cb_ref, lw_ref, lb_ref, o_ref, new_ref, u_scr,
                 *, tt, t_last, n_tiles):
    t = pl.program_id(1)

    @pl.when(t == 0)
    def _():
        u_scr[0:CONV_HALO, :] = buf_ref[...]

    @pl.when(t > 0)
    def _():
        u_scr[0:CONV_HALO, :] = u_scr[tt:tt + CONV_HALO, :]

    u_scr[CONV_HALO:CONV_HALO + tt, :] = za_ref[...] * jax.nn.sigmoid(zb_ref[...])

    off = CONV_HALO - (CONV_W - 1)
    for r0 in range(0, tt, CONV_ROWS):
        acc = u_scr[r0 + off:r0 + off + CONV_ROWS, :] * cw_ref[0:1, :]
        for w in range(1, CONV_W):
            acc = acc + u_scr[r0 + off + w:r0 + off + w + CONV_ROWS, :] * cw_ref[w:w + 1, :]
        y = _layer_norm_rows(acc + cb_ref[...], lw_ref[...], lb_ref[...])
        o_ref[r0:r0 + CONV_ROWS, :] = _silu(y).astype(o_ref.dtype)

    @pl.when(t == n_tiles - 1)
    def _():
        new_ref[...] = u_scr[t_last:t_last + CONV_HALO, :]


def conv_branch(z, col_a, col_b, conv_buf, cw, cb, lw, lb, seq_len, t_last, tt=128):
    n = z.shape[0]
    b = n // seq_len
    nt = seq_len // tt
    cdim = cw.shape[1]
    ja, jb = col_a // cdim, col_b // cdim
    return pl.pallas_call(
        functools.partial(_conv_kernel, tt=tt, t_last=t_last, n_tiles=nt),
        out_shape=(jax.ShapeDtypeStruct((n, cdim), BF16), jax.ShapeDtypeStruct((b, CONV_HALO, cdim), F32)),
        grid=(b, nt),
        in_specs=[
            pl.BlockSpec((tt, cdim), lambda bi, ti: (bi * nt + ti, ja)),
            pl.BlockSpec((tt, cdim), lambda bi, ti: (bi * nt + ti, jb)),
            pl.BlockSpec((None, CONV_HALO, cdim), lambda bi, ti: (bi, 0, 0)),
            pl.BlockSpec((CONV_HALO, cdim), lambda bi, ti: (0, 0)),
            pl.BlockSpec((1, cdim), lambda bi, ti: (0, 0)),
            pl.BlockSpec((1, cdim), lambda bi, ti: (0, 0)),
            pl.BlockSpec((1, cdim), lambda bi, ti: (0, 0)),
        ],
        out_specs=(pl.BlockSpec((tt, cdim), lambda bi, ti: (bi * nt + ti, 0)),
                   pl.BlockSpec((None, CONV_HALO, cdim), lambda bi, ti: (bi, 0, 0))),
        scratch_shapes=[pltpu.VMEM((CONV_HALO + tt, cdim), F32)],
        compiler_params=_cparams(("arbitrary", "arbitrary")),
        name="conv_branch",
    )(z, z, conv_buf, cw, cb.reshape(1, cdim), lw.reshape(1, cdim), lb.reshape(1, cdim))


def _tri_cumsum(tri_ref, x):
    hi, mid, lo = _split3(x)
    tri = tri_ref[...]
    return _dot(tri, hi) + _dot(tri, mid) + _dot(tri, lo)


LOG2E = 1.4426950408889634
N_BIAS_TERMS = 3


def _logf_kernel(zf_ref, bf_ref, tri_ref, lf_ref, c_ref, ct_ref, fq_ref, fk_ref, carry_ref, *, n_heads):
    @pl.when(pl.program_id(1) == 0)
    def _():
        carry_ref[...] = jnp.zeros_like(carry_ref)

    lf = _log_sigmoid(zf_ref[...] + bf_ref[...])
    c = _tri_cumsum(tri_ref, lf) + carry_ref[...]
    carry_ref[...] = c[LANES - 1:LANES, :]
    lf_ref[...] = lf
    c_ref[...] = c
    ct_ref[...] = jnp.transpose(c)[0:SUBLANES, :]

    lane = lax.broadcasted_iota(jnp.int32, c.shape, 1)
    ones_lanes = jnp.where(lane < 2 * N_BIAS_TERMS, 1.0, 0.0)
    for h in range(n_heads):
        col = jnp.sum(jnp.where(lane == h, c, 0.0), axis=1, keepdims=True) * LOG2E
        fq = ones_lanes
        fk = ones_lanes
        for i, t in enumerate(_split3(col)):
            t32 = t.astype(F32)
            fq = jnp.where(lane == i, t32, fq)
            fk = jnp.where(lane == N_BIAS_TERMS + i, -t32, fk)
        fq_ref[:, h * LANES:(h + 1) * LANES] = fq.astype(BF16)
        fk_ref[:, h * LANES:(h + 1) * LANES] = fk.astype(BF16)


def forget_cumsum(zf, b_forget_row, tri, seq_len, n_heads):
    n = zf.shape[0]
    b = n // seq_len
    nt = seq_len // LANES
    w = n_heads * LANES
    return pl.pallas_call(
        functools.partial(_logf_kernel, n_heads=n_heads),
        out_shape=(jax.ShapeDtypeStruct((n, LANES), F32), jax.ShapeDtypeStruct((n, LANES), F32),
                   jax.ShapeDtypeStruct((b, SUBLANES, seq_len), F32),
                   jax.ShapeDtypeStruct((n, w), BF16), jax.ShapeDtypeStruct((n, w), BF16)),
        grid=(b, nt),
        in_specs=[
            pl.BlockSpec((LANES, LANES), lambda bi, ti: (bi * nt + ti, 0)),
            pl.BlockSpec((1, LANES), lambda bi, ti: (0, 0)),
            pl.BlockSpec((LANES, LANES), lambda bi, ti: (0, 0)),
        ],
        out_specs=(pl.BlockSpec((LANES, LANES), lambda bi, ti: (bi * nt + ti, 0)),
                   pl.BlockSpec((LANES, LANES), lambda bi, ti: (bi * nt + ti, 0)),
                   pl.BlockSpec((None, SUBLANES, LANES), lambda bi, ti: (bi, 0, ti)),
                   pl.BlockSpec((LANES, w), lambda bi, ti: (bi * nt + ti, 0)),
                   pl.BlockSpec((LANES, w), lambda bi, ti: (bi * nt + ti, 0))),
        scratch_shapes=[pltpu.VMEM((1, LANES), F32)],
        compiler_params=_cparams(("arbitrary", "arbitrary")),
        name="forget_cumsum",
    )(zf, b_forget_row, tri)


FOX_COL_BLOCK = 2 * HEAD_DIM


def _fox_kernel(qi_ref, ki_ref, *refs, tq, tk, n_heads):
    nb = n_heads * HEAD_DIM // FOX_COL_BLOCK
    q_refs, k_refs, v_refs = refs[0:nb], refs[nb:2 * nb], refs[2 * nb:3 * nb]
    fq_ref, fk_ref, o_ref, m_scr, l_scr, acc_scr, qa_scr = refs[3 * nb:]
    step = pl.program_id(1)
    qi = qi_ref[step]
    ki = ki_ref[step]

    def head_cols(block_refs, h):
        lo = (h * HEAD_DIM) % FOX_COL_BLOCK
        return block_refs[h * HEAD_DIM // FOX_COL_BLOCK][:, lo:lo + HEAD_DIM]

    @pl.when(ki == 0)
    def _():
        m_scr[...] = jnp.full_like(m_scr, NEG_INF)
        l_scr[...] = jnp.zeros_like(l_scr)
        acc_scr[...] = jnp.zeros_like(acc_scr)
        for h in range(n_heads):
            qa_scr[h, :, 0:HEAD_DIM] = (head_cols(q_refs, h) * (HEAD_DIM ** -0.5 * LOG2E)).astype(BF16)
            qa_scr[h, :, HEAD_DIM:] = fq_ref[:, h * LANES:(h + 1) * LANES]

    def update(h, masked):
        ka = jnp.concatenate([head_cols(k_refs, h).astype(BF16), fk_ref[:, h * LANES:(h + 1) * LANES]], axis=1)
        vb = head_cols(v_refs, h).astype(BF16)
        s = _dot_nt(qa_scr[h], ka)
        if masked:
            row = lax.broadcasted_iota(jnp.int32, (tq, tk), 0)
            col = lax.broadcasted_iota(jnp.int32, (tq, tk), 1)
            s = jnp.where(col <= row, s, NEG_INF)
        m_prev = m_scr[h]
        m_new = jnp.maximum(m_prev, jnp.max(s, axis=1, keepdims=True))
        alpha = jnp.exp2(m_prev - m_new)
        p = jnp.exp2(s - m_new[:, 0:1])
        l_scr[h] = alpha * l_scr[h] + jnp.sum(p, axis=1, keepdims=True)
        acc_scr[h] = alpha * acc_scr[h] + _dot(p.astype(BF16), vb)
        m_scr[h] = m_new

    @pl.when(ki < qi)
    def _():
        for h in range(n_heads):
            update(h, False)

    @pl.when(ki == qi)
    def _():
        for h in range(n_heads):
            update(h, True)
            o_ref[:, h * HEAD_DIM:(h + 1) * HEAD_DIM] = (acc_scr[h] / l_scr[h]).astype(o_ref.dtype)


def fox_attention(z, col_q, col_k, col_v, fq, fk, n_heads, seq_len, tq=512):
    n = z.shape[0]
    b = n // seq_len
    tk = tq
    nq = seq_len // tq
    hd = HEAD_DIM
    w = n_heads * hd
    nb = w // FOX_COL_BLOCK
    pairs = [(qi, ki) for qi in range(nq) for ki in range(qi + 1)]
    qi_tab = jnp.asarray([p[0] for p in pairs], jnp.int32)
    ki_tab = jnp.asarray([p[1] for p in pairs], jnp.int32)

    def cols(col0, use_q):
        j0 = col0 // FOX_COL_BLOCK
        tile = tq if use_q else tk
        if use_q:
            return [pl.BlockSpec((tile, FOX_COL_BLOCK), lambda bi, s, qt, kt, j=j0 + i: (bi * nq + qt[s], j))
                    for i in range(nb)]
        return [pl.BlockSpec((tile, FOX_COL_BLOCK), lambda bi, s, qt, kt, j=j0 + i: (bi * nq + kt[s], j))
                for i in range(nb)]

    grid_spec = pltpu.PrefetchScalarGridSpec(
        num_scalar_prefetch=2,
        grid=(b, len(pairs)),
        in_specs=cols(col_q, True) + cols(col_k, False) + cols(col_v, False) + [
            pl.BlockSpec((tq, w), lambda bi, s, qt, kt: (bi * nq + qt[s], 0)),
            pl.BlockSpec((tk, w), lambda bi, s, qt, kt: (bi * nq + kt[s], 0)),
        ],
        out_specs=pl.BlockSpec((tq, w), lambda bi, s, qt, kt: (bi * nq + qt[s], 0)),
        scratch_shapes=[pltpu.VMEM((n_heads, tq, LANES), F32), pltpu.VMEM((n_heads, tq, LANES), F32),
                        pltpu.VMEM((n_heads, tq, hd), F32), pltpu.VMEM((n_heads, tq, hd + LANES), BF16)],
    )
    return pl.pallas_call(
        functools.partial(_fox_kernel, tq=tq, tk=tk, n_heads=n_heads),
        out_shape=jax.ShapeDtypeStruct((n, w), BF16),
        grid_spec=grid_spec,
        compiler_params=_cparams(("arbitrary", "arbitrary")),
        name="fox_attention",
    )(qi_tab, ki_tab, *([z] * (3 * nb)), fq, fk)


def _fox_paged_kernel(pt_ref, *refs, n_group, n_heads, t_pad, n_steps):
    del pt_ref
    g_ = n_group
    k_refs = refs[0:g_]
    v_refs = refs[g_:2 * g_]
    lf_refs = refs[2 * g_:3 * g_]
    (q_ref, kn_ref, vn_ref, cn_ref, ctn_ref, tri_ref, ones_ref, o_ref,
     m_scr, l_scr, acc_scr, carry_scr) = refs[3 * g_:]
    g = pl.program_id(1)
    scale = HEAD_DIM ** -0.5

    @pl.when(g == 0)
    def _():
        m_scr[...] = jnp.full_like(m_scr, NEG_INF)
        l_scr[...] = jnp.zeros_like(l_scr)
        acc_scr[...] = jnp.zeros_like(acc_scr)
        carry_scr[...] = jnp.zeros_like(carry_scr)

    lf_all = jnp.concatenate([r[...] for r in lf_refs], axis=0)
    hi, mid, lo = _split3(lf_all)
    tri = tri_ref[...]
    ones = ones_ref[...]
    after = _dot(hi, tri) + _dot(mid, tri) + _dot(lo, tri)
    total = _dot(hi, ones) + _dot(mid, ones) + _dot(lo, ones)
    carry = carry_scr[...]
    page_bias = []
    for i in range(g_):
        page_bias.append(carry + after[i * SUBLANES:(i + 1) * SUBLANES, :])
        carry = carry + total[i * SUBLANES:(i + 1) * SUBLANES, :]
    carry_scr[...] = carry

    cn = cn_ref[...]
    lane = lax.broadcasted_iota(jnp.int32, cn.shape, 1)

    def flash_update(h, s, v_list):
        m_prev = m_scr[h]
        m_new = jnp.maximum(m_prev, jnp.max(s, axis=1, keepdims=True))
        alpha = jnp.exp(m_prev - m_new)
        p32 = jnp.exp(s - m_new[:, 0:1])
        l_scr[h] = alpha * l_scr[h] + jnp.sum(p32, axis=1, keepdims=True)
        p = p32.astype(BF16)
        pv = _dot(p[:, 0:LANES], v_list[0])
        for i in range(1, len(v_list)):
            pv = pv + _dot(p[:, i * LANES:(i + 1) * LANES], v_list[i])
        acc_scr[h] = alpha * acc_scr[h] + pv
        m_scr[h] = m_new

    def scores(h):
        qh = q_ref[:, h * HEAD_DIM:(h + 1) * HEAD_DIM].astype(BF16)
        cq = jnp.sum(jnp.where(lane == h, cn, 0.0), axis=1, keepdims=True)
        s_list = []
        for i in range(g_):
            kh = k_refs[i][h].astype(BF16)
            s_list.append(_dot_nt(qh, kh) * scale + (cq + page_bias[i][h:h + 1, :]))
        return jnp.concatenate(s_list, axis=1)

    s_all = [scores(h) for h in range(n_heads)]
    for h in range(n_heads):
        flash_update(h, s_all[h], [v_refs[i][h].astype(BF16) for i in range(g_)])

    @pl.when(g == n_steps - 1)
    def _():
        row = lax.broadcasted_iota(jnp.int32, (t_pad, LANES), 0)
        col = lax.broadcasted_iota(jnp.int32, (t_pad, LANES), 1)
        for h in range(n_heads):
            hs = slice(h * HEAD_DIM, (h + 1) * HEAD_DIM)
            qh = q_ref[:, hs].astype(BF16)
            cq = jnp.sum(jnp.where(lane == h, cn, 0.0), axis=1, keepdims=True)
            s = _dot_nt(qh, kn_ref[:, hs].astype(BF16)) * scale + (cq - ctn_ref[h:h + 1, :])
            s = jnp.where(col <= row, s, NEG_INF)
            flash_update(h, s, [vn_ref[:, hs].astype(BF16)])
            o_ref[:, hs] = acc_scr[h] / l_scr[h]


def fox_paged_attention(q, k_new, v_new, c_new, ct_new, cache_k, cache_v, logf_t, page_table, layer,
                        tri_after, ones_mat, n_group=16, t_pad=SUBLANES):
    b, n_pages = page_table.shape
    n_heads = cache_k.shape[2]
    page = cache_k.shape[3]
    hd = HEAD_DIM
    w = n_heads * hd
    ng = n_pages // n_group

    def page_of(i):
        return lambda bi, gi, pt: (pt[bi, n_pages - 1 - (gi * n_group + i)], layer, 0, 0, 0)

    def lf_of(i):
        return lambda bi, gi, pt: (layer, pt[bi, n_pages - 1 - (gi * n_group + i)], 0, 0)

    kv_specs = [pl.BlockSpec((None, None, n_heads, page, hd), page_of(i)) for i in range(n_group)]
    lf_specs = [pl.BlockSpec((None, None, SUBLANES, page), lf_of(i)) for i in range(n_group)]
    grid_spec = pltpu.PrefetchScalarGridSpec(
        num_scalar_prefetch=1,
        grid=(b, ng),
        in_specs=kv_specs + kv_specs + lf_specs + [
            pl.BlockSpec((None, t_pad, w), lambda bi, gi, pt: (bi, 0, 0)),
            pl.BlockSpec((None, LANES, w), lambda bi, gi, pt: (bi, 0, 0)),
            pl.BlockSpec((None, LANES, w), lambda bi, gi, pt: (bi, 0, 0)),
            pl.BlockSpec((t_pad, LANES), lambda bi, gi, pt: (bi * (LANES // t_pad), 0)),
            pl.BlockSpec((None, SUBLANES, LANES), lambda bi, gi, pt: (bi, 0, 0)),
            pl.BlockSpec((LANES, LANES), lambda bi, gi, pt: (0, 0)),
            pl.BlockSpec((LANES, LANES), lambda bi, gi, pt: (0, 0)),
        ],
        out_specs=pl.BlockSpec((None, t_pad, w), lambda bi, gi, pt: (bi, 0, 0)),
        scratch_shapes=[pltpu.VMEM((n_heads, t_pad, LANES), F32), pltpu.VMEM((n_heads, t_pad, LANES), F32),
                        pltpu.VMEM((n_heads, t_pad, hd), F32), pltpu.VMEM((SUBLANES, LANES), F32)],
    )
    return pl.pallas_call(
        functools.partial(_fox_paged_kernel, n_group=n_group, n_heads=n_heads, t_pad=t_pad, n_steps=ng),
        out_shape=jax.ShapeDtypeStruct((b, t_pad, w), F32),
        grid_spec=grid_spec,
        compiler_params=_cparams(("arbitrary", "arbitrary")),
        name="fox_paged_attention",
    )(page_table, *([cache_k] * n_group), *([cache_v] * n_group), *([logf_t] * n_group),
      q, k_new, v_new, c_new, ct_new, tri_after, ones_mat)


def _router_kernel(x_ref, sc_ref, sh_ref, wr_ref, br_ref, h_ref, info_ref, *, n_experts):
    h = x_ref[...] * (1.0 + sc_ref[...]) + sh_ref[...]
    h_ref[...] = h
    logits = _dot(h.astype(BF16), wr_ref[...].astype(BF16)) + br_ref[...]
    lane = lax.broadcasted_iota(jnp.int32, logits.shape, 1)
    neg = -jnp.inf
    l1 = jnp.where(lane < n_experts, logits, neg)
    m1 = jnp.max(l1, axis=1, keepdims=True)
    i1 = jnp.min(jnp.where(l1 == m1, lane, LANES), axis=1, keepdims=True)
    l2 = jnp.where(lane == i1, neg, l1)
    m2 = jnp.max(l2, axis=1, keepdims=True)
    i2 = jnp.min(jnp.where(l2 == m2, lane, LANES), axis=1, keepdims=True)
    e = jnp.exp(m2 - m1)
    den = 1.0 + e
    p1 = 1.0 / den
    p2 = e / den
    info = jnp.where(lane == 0, i1.astype(F32),
                     jnp.where(lane == 1, i2.astype(F32),
                               jnp.where(lane == 2, p1, jnp.where(lane == 3, p2, 0.0))))
    info_ref[...] = info


def moe_route(x, sc, sh, w_router_pad, b_router_pad, n_experts, tm, rows_per_seq):
    n, d = x.shape
    return pl.pallas_call(
        functools.partial(_router_kernel, n_experts=n_experts),
        out_shape=(jax.ShapeDtypeStruct((n, d), F32), jax.ShapeDtypeStruct((n, LANES), F32)),
        grid=(n // tm,),
        in_specs=[
            pl.BlockSpec((tm, d), lambda i: (i, 0)),
            _mod_spec(sc, tm, rows_per_seq, 1),
            _mod_spec(sh, tm, rows_per_seq, 1),
            pl.BlockSpec((d, LANES), lambda i: (0, 0)),
            pl.BlockSpec((1, LANES), lambda i: (0, 0)),
        ],
        out_specs=(pl.BlockSpec((tm, d), lambda i: (i, 0)), pl.BlockSpec((tm, LANES), lambda i: (i, 0))),
        compiler_params=_cparams(("arbitrary",)),
        name="moe_router",
    )(x, sc, sh, w_router_pad, b_router_pad)


def moe_plan(experts, n_experts, tm):
    n = experts.shape[0]
    na = n * TOP_K
    n_tiles = pl.cdiv(na, tm) + n_experts
    flat_e = experts.reshape(na)
    onehot = (flat_e[:, None] == jnp.arange(n_experts, dtype=jnp.int32)[None, :]).astype(jnp.int32)
    incl = jnp.cumsum(onehot, axis=0)
    rank = jnp.sum((incl - onehot) * onehot, axis=1)
    counts = incl[-1]
    tiles_e = (counts + tm - 1) // tm
    tile_end = jnp.cumsum(tiles_e)
    tile_off = tile_end - tiles_e
    n_used = tile_end[-1:]
    dest = (jnp.sum(tile_off[None, :] * onehot, axis=1) * tm + rank).astype(jnp.int32)
    src_token = jnp.zeros((n_tiles * tm,), jnp.int32).at[dest].set(jnp.arange(na, dtype=jnp.int32) // TOP_K)
    t = jnp.arange(n_tiles, dtype=jnp.int32)
    tile_expert = jnp.sum((t[:, None] >= tile_end[None, :]).astype(jnp.int32), axis=1)
    tile_expert = jnp.minimum(tile_expert, n_experts - 1).astype(jnp.int32)
    return dest, src_token, tile_expert, n_used.astype(jnp.int32)


def _row_copy(src_ref, row, dst_ref, slot, sem):
    return pltpu.make_async_copy(src_ref.at[pl.ds(row, 1), :], dst_ref.at[pl.ds(slot, 1), :], sem)


GATHER_UNROLL = 8


def _moe_gather_kernel(src_ref, nused_ref, h_ref, o_ref, buf, sem, *, tm):
    t = pl.program_id(0)

    @pl.when(t < nused_ref[0])
    def _():
        base = t * tm

        def issue(r8, carry):
            for u in range(GATHER_UNROLL):
                r = r8 * GATHER_UNROLL + u
                _row_copy(h_ref, src_ref[base + r], buf, r, sem).start(priority=u % 2)
            return carry

        lax.fori_loop(0, tm // GATHER_UNROLL, issue, 0)

        def drain(r, carry):
            _row_copy(h_ref, 0, buf, r, sem).wait()
            return carry

        lax.fori_loop(0, tm, drain, 0, unroll=GATHER_UNROLL)
        o_ref[...] = buf[...].astype(BF16)

    @pl.when(t >= nused_ref[0])
    def _():
        o_ref[...] = jnp.zeros_like(o_ref)


def moe_gather(h, src_token, n_used, tm):
    s = src_token.shape[0]
    d = h.shape[1]
    n_tiles = s // tm
    grid_spec = pltpu.PrefetchScalarGridSpec(
        num_scalar_prefetch=2,
        grid=(n_tiles,),
        in_specs=[pl.BlockSpec(memory_space=pl.ANY)],
        out_specs=pl.BlockSpec((tm, d), lambda t, src, nu: (t, 0)),
        scratch_shapes=[pltpu.VMEM((tm, d), F32), pltpu.SemaphoreType.DMA(())],
    )
    return pl.pallas_call(
        functools.partial(_moe_gather_kernel, tm=tm),
        out_shape=jax.ShapeDtypeStruct((s, d), BF16),
        grid_spec=grid_spec,
        compiler_params=_cparams(("arbitrary",)),
        name="moe_gather",
    )(src_token, n_used, h)


def _new_expert(te_ref, nused_ref, t):
    last = nused_ref[0] - 1
    cur = jnp.minimum(t, last)
    prev = jnp.maximum(cur - 1, 0)
    return jnp.logical_and(t <= last, jnp.logical_or(t == 0, te_ref[cur] != te_ref[prev]))


def _moe_up_kernel(te_ref, nused_ref, x_ref, wg_ref, wu_ref, a_ref, wgb_scr, wub_scr):
    t = pl.program_id(1)

    @pl.when(_new_expert(te_ref, nused_ref, t))
    def _():
        wgb_scr[...] = wg_ref[...].astype(BF16)
        wub_scr[...] = wu_ref[...].astype(BF16)

    @pl.when(t < nused_ref[0])
    def _():
        xb = x_ref[...]
        gate = _dot(xb, wgb_scr[...])
        up = _dot(xb, wub_scr[...])
        a_ref[...] = (_silu(gate) * up).astype(a_ref.dtype)

    @pl.when(t >= nused_ref[0])
    def _():
        a_ref[...] = jnp.zeros_like(a_ref)


def moe_up(xs, w_gate, w_up, idx, tile_expert, n_used, tm, tf=512):
    s, d = xs.shape
    ff = w_gate.shape[3]
    n_tiles = s // tm

    def tile(t, nu):
        return jnp.minimum(t, nu[0] - 1)

    grid_spec = pltpu.PrefetchScalarGridSpec(
        num_scalar_prefetch=2,
        grid=(pl.cdiv(ff, tf), n_tiles),
        in_specs=[
            pl.BlockSpec((tm, d), lambda f, t, te, nu: (tile(t, nu), 0)),
            pl.BlockSpec((None, None, d, tf), lambda f, t, te, nu: (idx, te[tile(t, nu)], 0, f)),
            pl.BlockSpec((None, None, d, tf), lambda f, t, te, nu: (idx, te[tile(t, nu)], 0, f)),
        ],
        out_specs=pl.BlockSpec((tm, tf), lambda f, t, te, nu: (t, f)),
        scratch_shapes=[pltpu.VMEM((d, tf), BF16), pltpu.VMEM((d, tf), BF16)],
    )
    return pl.pallas_call(
        _moe_up_kernel,
        out_shape=jax.ShapeDtypeStruct((s, ff), BF16),
        grid_spec=grid_spec,
        compiler_params=_cparams(("arbitrary", "arbitrary")),
        name="moe_up",
    )(tile_expert, n_used, xs, w_gate, w_up)


def _moe_down_kernel(te_ref, nused_ref, a_ref, w_ref, o_ref, wb_scr):
    t = pl.program_id(1)

    @pl.when(_new_expert(te_ref, nused_ref, t))
    def _():
        wb_scr[...] = w_ref[...].astype(BF16)

    @pl.when(t < nused_ref[0])
    def _():
        o_ref[...] = _dot(a_ref[...], wb_scr[...])

    @pl.when(t >= nused_ref[0])
    def _():
        o_ref[...] = jnp.zeros_like(o_ref)


def moe_down(a, w_down, idx, tile_expert, n_used, tm, tn=512):
    s, ff = a.shape
    d = w_down.shape[3]
    n_tiles = s // tm

    def tile(t, nu):
        return jnp.minimum(t, nu[0] - 1)

    grid_spec = pltpu.PrefetchScalarGridSpec(
        num_scalar_prefetch=2,
        grid=(d // tn, n_tiles),
        in_specs=[
            pl.BlockSpec((tm, ff), lambda j, t, te, nu: (tile(t, nu), 0)),
            pl.BlockSpec((None, None, ff, tn), lambda j, t, te, nu: (idx, te[tile(t, nu)], 0, j)),
        ],
        out_specs=pl.BlockSpec((tm, tn), lambda j, t, te, nu: (t, j)),
        scratch_shapes=[pltpu.VMEM((ff, tn), BF16)],
    )
    return pl.pallas_call(
        _moe_down_kernel,
        out_shape=jax.ShapeDtypeStruct((s, d), F32),
        grid_spec=grid_spec,
        compiler_params=_cparams(("arbitrary", "arbitrary")),
        name="moe_down",
    )(tile_expert, n_used, a, w_down)


def _moe_combine_kernel(dest_ref, y_ref, info_ref, x_ref, g_ref, w_ref, b_ref, o_ref, buf0, buf1, sem,
                        *, tm, row0, alpha):
    base = (row0 + pl.program_id(0) * tm) * TOP_K

    def issue(r8, carry):
        for u in range(GATHER_UNROLL):
            r = r8 * GATHER_UNROLL + u
            _row_copy(y_ref, dest_ref[base + TOP_K * r], buf0, r, sem).start(priority=0)
            _row_copy(y_ref, dest_ref[base + TOP_K * r + 1], buf1, r, sem).start(priority=1)
        return carry

    lax.fori_loop(0, tm // GATHER_UNROLL, issue, 0)

    def drain(r, carry):
        _row_copy(y_ref, 0, buf0, r, sem).wait()
        _row_copy(y_ref, 0, buf1, r, sem).wait()
        return carry

    lax.fori_loop(0, tm, drain, 0, unroll=GATHER_UNROLL)
    info = info_ref[...]
    f = info[:, 2:3] * buf0[...] + info[:, 3:4] * buf1[...]
    y = alpha * x_ref[...] + g_ref[...] * f
    o_ref[...] = _layer_norm_rows(y, w_ref[...], b_ref[...])


def moe_combine_ln(y_sorted, dest, info, x, g, w, b, row0, alpha, tm, rows_per_seq):
    n, d = x.shape
    gspec = _mod_spec(g, tm, rows_per_seq, 1)
    gmap = gspec.index_map
    grid_spec = pltpu.PrefetchScalarGridSpec(
        num_scalar_prefetch=1,
        grid=(n // tm,),
        in_specs=[
            pl.BlockSpec(memory_space=pl.ANY),
            pl.BlockSpec((tm, LANES), lambda i, dst: (i, 0)),
            pl.BlockSpec((tm, d), lambda i, dst: (i, 0)),
            pl.BlockSpec(gspec.block_shape, lambda i, dst: gmap(i)),
            pl.BlockSpec((1, d), lambda i, dst: (0, 0)),
            pl.BlockSpec((1, d), lambda i, dst: (0, 0)),
        ],
        out_specs=pl.BlockSpec((tm, d), lambda i, dst: (i, 0)),
        scratch_shapes=[pltpu.VMEM((tm, d), F32), pltpu.VMEM((tm, d), F32), pltpu.SemaphoreType.DMA(())],
    )
    return pl.pallas_call(
        functools.partial(_moe_combine_kernel, tm=tm, row0=row0, alpha=alpha),
        out_shape=jax.ShapeDtypeStruct((n, d), F32),
        grid_spec=grid_spec,
        compiler_params=_cparams(("arbitrary",)),
        name="moe_combine_ln",
    )(dest, y_sorted, info, x, g, w.reshape(1, d), b.reshape(1, d))


def _kv_pack_kernel(*refs, depth, n_heads):
    nb = n_heads * HEAD_DIM // FOX_COL_BLOCK
    k_out, v_out = refs[2 * depth * nb:]
    for l in range(depth):
        for which, out in ((0, k_out), (1, v_out)):
            blocks = refs[(2 * l + which) * nb:(2 * l + which + 1) * nb]
            for h in range(n_heads):
                lo = (h * HEAD_DIM) % FOX_COL_BLOCK
                out[l, h] = blocks[h * HEAD_DIM // FOX_COL_BLOCK][:, lo:lo + HEAD_DIM]


def kv_pack(z_layers, col_k, col_v, n_heads, seq_len, tt=256):
    depth = len(z_layers)
    n = z_layers[0].shape[0]
    b = n // seq_len
    nt = seq_len // tt
    nb = n_heads * HEAD_DIM // FOX_COL_BLOCK

    def cols(col0):
        j0 = col0 // FOX_COL_BLOCK
        return [pl.BlockSpec((tt, FOX_COL_BLOCK), lambda bi, ti, j=j0 + i: (bi * nt + ti, j)) for i in range(nb)]

    in_specs, operands = [], []
    for z in z_layers:
        in_specs += cols(col_k) + cols(col_v)
        operands += [z] * (2 * nb)
    out_sds = jax.ShapeDtypeStruct((b, depth, n_heads, seq_len, HEAD_DIM), F32)
    out_spec = pl.BlockSpec((None, depth, n_heads, tt, HEAD_DIM), lambda bi, ti: (bi, 0, 0, ti, 0))
    return pl.pallas_call(
        functools.partial(_kv_pack_kernel, depth=depth, n_heads=n_heads),
        out_shape=(out_sds, out_sds),
        grid=(b, nt),
        in_specs=in_specs,
        out_specs=(out_spec, out_spec),
        compiler_params=_cparams(("arbitrary", "arbitrary")),
        name="kv_pack",
    )(*operands)


PROMPT_TM = 1024
PROMPT_WS_TM = 2048
PROJ_IN_TN = 1280
PROJ_OUT_TN = 512
PROMPT_LN_TM = 512
MOE_TM = 512
MOE_COMBINE_TM = 256


def _pad_axis(a, axis, size):
    pad = [(0, 0)] * a.ndim
    pad[axis] = (0, size - a.shape[axis])
    return jnp.pad(a, pad)


def kernel(x_prompt, x_sample, cache_k, cache_v, cache_logf, state_ret, state_conv, page_table, c_prompt, c_sample, w_ada, b_ada, w_in, b_forget, ret_gn_w, conv_w, conv_b, conv_ln_w, conv_ln_b, w_out, ln1_w, ln1_b, ln2_w, ln2_b, ffn_w_gate, ffn_w_up, ffn_w_down, moe_router, moe_router_b, moe_w_gate, moe_w_up, moe_w_down):
    bp, tp, d = x_prompt.shape
    bs, ts, _ = x_sample.shape
    depth = w_in.shape[0]
    ret_heads = state_ret.shape[2]
    fox_heads = cache_k.shape[3]
    page = cache_k.shape[2]
    conv_dim = state_conv.shape[3]
    n_experts = moe_router.shape[2]
    hd = HEAD_DIM
    ret_w, fox_w = ret_heads * hd, fox_heads * hd
    o_ca = 4 * ret_w
    o_cb = o_ca + conv_dim
    o_fq = o_cb + conv_dim
    o_fk = o_fq + fox_w
    o_fv = o_fk + fox_w
    n_main = o_fv + fox_w
    alpha = (2 * depth) ** 0.25
    past_len = page_table.shape[1] * page
    tpad = RET_CHUNK
    np_rows, ns_rows = bp * tp, bs * ts

    w_forget = _pad_axis(w_in[:, :, n_main:], 2, LANES)
    bf_rows = _pad_axis(b_forget, 1, LANES).reshape(depth, 1, LANES)
    router_w = _pad_axis(moe_router, 2, LANES)
    router_b = _pad_axis(moe_router_b, 1, LANES).reshape(-1, 1, LANES)
    conv_w_pad = _pad_axis(conv_w, 1, CONV_HALO)
    logf_t = _pad_axis(jnp.transpose(cache_logf, (1, 0, 3, 2)), 2, SUBLANES)
    cache_k_hm = jnp.transpose(cache_k, (0, 1, 3, 2, 4))
    cache_v_hm = jnp.transpose(cache_v, (0, 1, 3, 2, 4))
    tri_incl = jnp.tril(jnp.ones((LANES, LANES), BF16))
    tri_after = jnp.tril(jnp.ones((page, page), BF16), -1)
    ones_mat = jnp.ones((page, page), BF16)
    tabs_p = retention_tables(ret_heads, tp, 0, tp)
    tabs_s = retention_tables(ret_heads, ts, past_len, tpad)
    r0_p = jnp.zeros((bp, ret_heads, hd, hd), F32)
    buf0_p = jnp.zeros((bp, CONV_HALO, conv_dim), F32)

    c_rows = _pad_axis(jnp.concatenate([c_prompt, c_sample], axis=0), 0, 2 * SUBLANES)
    mod = ada_modulation(c_rows, w_ada, b_ada)

    xp = x_prompt.reshape(np_rows, d)
    xs = x_sample.reshape(ns_rows, d)
    rets_p, rets_s, convs_p, convs_s, zs_p, ks_s, vs_s, lfs_p, lfs_s = ([] for _ in range(9))

    def unpad(a):
        return a.reshape(bs, tpad, -1)[:, :ts].reshape(ns_rows, -1)

    def prompt_mod(l):
        return [mod[l, :bp, i * d:(i + 1) * d].reshape(bp, 1, d) for i in range(6)]

    sh1p, sc1p = prompt_mod(0)[:2]
    hb = modulate_cast(xp, sc1p, sh1p, PROMPT_LN_TM, tp)

    for l in range(depth):
        parts = [mod[l, :, i * d:(i + 1) * d] for i in range(6)]
        sh1p, sc1p, g1p, sh2p, sc2p, g2p = prompt_mod(l)
        sh1s, sc1s, g1s, sh2s, sc2s, g2s = (jnp.repeat(p[bp:bp + bs], ts, axis=0) for p in parts)
        next_in = prompt_mod(l + 1)[1::-1] if l + 1 < depth else None

        z = matmul_ws([hb], w_in, (l,), n_main, PROJ_IN_TN, PROMPT_TM, "proj_in_ws")
        zf = matmul_ws([hb], w_forget, (l,), LANES, LANES, PROMPT_TM, "proj_forget_ws")
        ret_o, r_new = retention(z, tabs_p, ret_gn_w[l], r0_p, ret_heads, tp)
        conv_o, conv_new = conv_branch(z, o_ca, o_cb, buf0_p, conv_w_pad[l], conv_b[l], conv_ln_w[l],
                                       conv_ln_b[l], tp, RET_CHUNK)
        lf, _, _, bias_q, bias_k = forget_cumsum(zf, bf_rows[l], tri_incl, tp, fox_heads)
        fox_o = fox_attention(z, o_fq, o_fk, o_fv, bias_q, bias_k, fox_heads, tp)
        m = matmul_ws([ret_o, conv_o, fox_o], w_out, (l,), d, PROJ_OUT_TN, PROMPT_WS_TM, "proj_out_ws")
        if l % 2 == 0:
            x1p, hb2 = ln_residual(xp, m, g1p, ln1_w[l], ln1_b[l], alpha, PROMPT_LN_TM, tp, next_mod=(sc2p, sh2p))
        else:
            x1p = ln_residual(xp, m, g1p, ln1_w[l], ln1_b[l], alpha, PROMPT_LN_TM, tp)
        rets_p.append(r_new)
        convs_p.append(conv_new[:, CONV_HALO - (CONV_W - 1):])
        zs_p.append(z)
        lfs_p.append(lf[:, :fox_heads].reshape(bp, tp, fox_heads))

        zs, zfs = proj_in(xs, sc1s, sh1s, w_in, w_forget, l, n_main, ns_rows, ts, tn=640)
        zs_pad = _pad_axis(zs.reshape(bs, ts, n_main), 1, tpad)
        z2 = zs_pad.reshape(bs * tpad, n_main)
        zf2 = _pad_axis(zfs.reshape(bs, ts, LANES), 1, tpad).reshape(bs * tpad, LANES)
        ret_s, r_new_s = retention(z2, tabs_s, ret_gn_w[l], state_ret[:, l], ret_heads, tpad)
        buf_s = jnp.pad(state_conv[:, l], ((0, 0), (CONV_HALO - (CONV_W - 1), 0), (0, 0)))
        conv_s, conv_new_s = conv_branch(z2, o_ca, o_cb, buf_s, conv_w_pad[l], conv_b[l], conv_ln_w[l],
                                         conv_ln_b[l], tpad, ts)
        lf_s, csum_s, csum_t_s, _, _ = forget_cumsum(zf2, bf_rows[l], tri_incl, tpad, fox_heads)
        fox_s = fox_paged_attention(zs_pad[:, :SUBLANES, o_fq:o_fk], zs_pad[:, :, o_fk:o_fv],
                                    zs_pad[:, :, o_fv:n_main], csum_s, csum_t_s, cache_k_hm, cache_v_hm, logf_t,
                                    page_table, l, tri_after, ones_mat)
        fox_s = fox_s[:, :ts].reshape(ns_rows, fox_w)
        m_s = proj_out(unpad(ret_s), unpad(conv_s), fox_s, w_out, l, ns_rows)
        x1s = ln_residual(xs, m_s, g1s, ln1_w[l], ln1_b[l], alpha, ns_rows, ts)
        rets_s.append(r_new_s)
        convs_s.append(conv_new_s[:, CONV_HALO - (CONV_W - 1):])
        ks_s.append(zs[:, o_fk:o_fv].reshape(bs, ts, fox_heads, hd))
        vs_s.append(zs[:, o_fv:n_main].reshape(bs, ts, fox_heads, hd))
        lfs_s.append(unpad(lf_s)[:, :fox_heads].reshape(bs, ts, fox_heads))

        i = l // 2
        if l % 2 == 0:
            a = ffn_up_ws(hb2, ffn_w_gate, ffn_w_up, i, PROMPT_WS_TM)
            y = ffn_down(a, ffn_w_down, i, PROMPT_TM)
            if next_in is None:
                xp = ln_residual(x1p, y, g2p, ln2_w[l], ln2_b[l], alpha, PROMPT_LN_TM, tp)
            else:
                xp, hb = ln_residual(x1p, y, g2p, ln2_w[l], ln2_b[l], alpha, PROMPT_LN_TM, tp, next_mod=next_in)
            a_s = ffn_up(x1s, sc2s, sh2s, ffn_w_gate, ffn_w_up, i, ns_rows, ts)
            y_s = ffn_down(a_s, ffn_w_down, i, ns_rows)
            xs = ln_residual(x1s, y_s, g2s, ln2_w[l], ln2_b[l], alpha, ns_rows, ts)
        else:
            h_p, info_p = moe_route(x1p, sc2p, sh2p, router_w[i], router_b[i], n_experts, PROMPT_LN_TM, tp)
            h_s, info_s = moe_route(x1s, sc2s, sh2s, router_w[i], router_b[i], n_experts, ns_rows, ts)
            experts = jnp.concatenate([info_p[:, :TOP_K], info_s[:, :TOP_K]], axis=0).astype(jnp.int32)
            dest, src_token, tile_expert, n_used = moe_plan(experts, n_experts, MOE_TM)
            x_sorted = moe_gather(jnp.concatenate([h_p, h_s], axis=0), src_token, n_used, MOE_TM)
            a = moe_up(x_sorted, moe_w_gate, moe_w_up, i, tile_expert, n_used, MOE_TM)
            y = moe_down(a, moe_w_down, i, tile_expert, n_used, MOE_TM)
            xp = moe_combine_ln(y, dest, info_p, x1p, g2p, ln2_w[l], ln2_b[l], 0, alpha, MOE_COMBINE_TM, tp)
            xs = moe_combine_ln(y, dest, info_s, x1s, g2s, ln2_w[l], ln2_b[l], np_rows, alpha, ns_rows, ts)
            if next_in is not None:
                hb = modulate_cast(xp, next_in[0], next_in[1], PROMPT_LN_TM, tp)

    st = lambda lst: jnp.stack(lst, axis=1)
    k_hm, v_hm = kv_pack(zs_p, o_fk, o_fv, fox_heads, tp)
    k_p = jnp.transpose(k_hm, (0, 1, 3, 2, 4))
    v_p = jnp.transpose(v_hm, (0, 1, 3, 2, 4))
    return (xp.reshape(bp, tp, d), xs.reshape(bs, ts, d), st(rets_p), st(rets_s), st(convs_p), st(convs_s),
            k_p, st(ks_s), v_p, st(vs_s), st(lfs_p), st(lfs_s))
```

```python
import functools

import jax
import jax.numpy as jnp
from jax import lax
from jax.experimental import pallas as pl
from jax.experimental.pallas import tpu as pltpu

F32 = jnp.float32
BF16 = jnp.bfloat16

LANES = 128
SUBLANES = 8
VMEM_LIMIT_BYTES = 56 * 1024 * 1024

HEAD_DIM = 128
CONV_W = 31
CONV_HALO = 32
RET_CHUNK = 128
TOP_K = 2
LN_EPS = 1e-5
NEG_INF = -1e30
ROPE_BASE = 10000.0


def _cparams(sem, vmem=VMEM_LIMIT_BYTES):
    return pltpu.CompilerParams(dimension_semantics=sem, vmem_limit_bytes=vmem)


def _silu(x):
    return x * jax.nn.sigmoid(x)


def _dot(a, b):
    return jnp.dot(a, b, preferred_element_type=F32)


def _dot_nt(a, b):
    return lax.dot_general(a, b, (((1,), (1,)), ((), ())), preferred_element_type=F32)


def _split3(x):
    hi = x.astype(BF16)
    r1 = x - hi.astype(F32)
    mid = r1.astype(BF16)
    lo = (r1 - mid.astype(F32)).astype(BF16)
    return hi, mid, lo


def _log_sigmoid(x):
    return -(jnp.maximum(-x, 0.0) + jnp.log1p(jnp.exp(-jnp.abs(x))))


def _layer_norm_rows(y, w, b):
    mu = jnp.mean(y, axis=-1, keepdims=True)
    d = y - mu
    var = jnp.mean(d * d, axis=-1, keepdims=True)
    return d * lax.rsqrt(var + LN_EPS) * w + b


def _ada_kernel(c_ref, w_ref, b_ref, o_ref):
    cs = _silu(c_ref[...]).astype(BF16)
    o_ref[...] = _dot(cs, w_ref[...].astype(BF16)) + b_ref[...]


def ada_modulation(c_rows, w_ada, b_ada, tn=1024):
    depth, d, n6 = w_ada.shape
    r = c_rows.shape[0]
    return pl.pallas_call(
        _ada_kernel,
        out_shape=jax.ShapeDtypeStruct((depth, r, n6), F32),
        grid=(depth, n6 // tn),
        in_specs=[
            pl.BlockSpec((r, d), lambda l, j: (0, 0)),
            pl.BlockSpec((None, d, tn), lambda l, j: (l, 0, j)),
            pl.BlockSpec((None, 1, tn), lambda l, j: (l, 0, j)),
        ],
        out_specs=pl.BlockSpec((None, r, tn), lambda l, j: (l, 0, j)),
        compiler_params=_cparams(("arbitrary", "arbitrary")),
        name="ada_modulation",
    )(c_rows, w_ada, b_ada.reshape(depth, 1, n6))


def _mod_spec(arr, tm, rows_per_seq, grid_rank):
    d = arr.shape[-1]
    if arr.ndim == 3:
        assert rows_per_seq % tm == 0
        per = rows_per_seq // tm
        if grid_rank == 1:
            return pl.BlockSpec((None, 1, d), lambda i: (i // per, 0, 0))
        return pl.BlockSpec((None, 1, d), lambda i, j: (i // per, 0, 0))
    if grid_rank == 1:
        return pl.BlockSpec((tm, d), lambda i: (i, 0))
    return pl.BlockSpec((tm, d), lambda i, j: (i, 0))


def _proj_in_kernel(x_ref, sc_ref, sh_ref, w_ref, wf_ref, z_ref, zf_ref, xb_ref):
    @pl.when(pl.program_id(1) == 0)
    def _():
        h = x_ref[...] * (1.0 + sc_ref[...]) + sh_ref[...]
        xb_ref[...] = h.astype(BF16)
        zf_ref[...] = _dot(xb_ref[...], wf_ref[...].astype(BF16))

    z_ref[...] = _dot(xb_ref[...], w_ref[...].astype(BF16))


def proj_in(x, sc, sh, w_in, w_forget, layer, n_main, tm, rows_per_seq, tn=512):
    n, d = x.shape
    return pl.pallas_call(
        _proj_in_kernel,
        out_shape=(jax.ShapeDtypeStruct((n, n_main), F32), jax.ShapeDtypeStruct((n, LANES), F32)),
        grid=(n // tm, n_main // tn),
        in_specs=[
            pl.BlockSpec((tm, d), lambda i, j: (i, 0)),
            _mod_spec(sc, tm, rows_per_seq, 2),
            _mod_spec(sh, tm, rows_per_seq, 2),
            pl.BlockSpec((None, d, tn), lambda i, j: (layer, 0, j)),
            pl.BlockSpec((None, d, LANES), lambda i, j: (layer, 0, 0)),
        ],
        out_specs=(pl.BlockSpec((tm, tn), lambda i, j: (i, j)),
                   pl.BlockSpec((tm, LANES), lambda i, j: (i, 0))),
        scratch_shapes=[pltpu.VMEM((tm, d), BF16)],
        compiler_params=_cparams(("arbitrary", "arbitrary")),
        name="proj_in",
    )(x, sc, sh, w_in, w_forget)


def _proj_out_kernel(a_ref, b_ref, c_ref, w_ref, o_ref, xb_ref):
    @pl.when(pl.program_id(1) == 0)
    def _():
        wa, wb = a_ref.shape[1], b_ref.shape[1]
        xb_ref[:, 0:wa] = a_ref[...].astype(BF16)
        xb_ref[:, wa:wa + wb] = b_ref[...].astype(BF16)
        xb_ref[:, wa + wb:] = c_ref[...].astype(BF16)

    o_ref[...] = _dot(xb_ref[...], w_ref[...].astype(BF16))


def proj_out(a, b, c, w_out, layer, tm, tn=512):
    n = a.shape[0]
    d = w_out.shape[1]
    dout = w_out.shape[2]
    return pl.pallas_call(
        _proj_out_kernel,
        out_shape=jax.ShapeDtypeStruct((n, dout), F32),
        grid=(n // tm, dout // tn),
        in_specs=[
            pl.BlockSpec((tm, a.shape[1]), lambda i, j: (i, 0)),
            pl.BlockSpec((tm, b.shape[1]), lambda i, j: (i, 0)),
            pl.BlockSpec((tm, c.shape[1]), lambda i, j: (i, 0)),
            pl.BlockSpec((None, d, tn), lambda i, j: (layer, 0, j)),
        ],
        out_specs=pl.BlockSpec((tm, tn), lambda i, j: (i, j)),
        scratch_shapes=[pltpu.VMEM((tm, d), BF16)],
        compiler_params=_cparams(("arbitrary", "arbitrary")),
        name="proj_out",
    )(a, b, c, w_out)


def _ln_res_kernel(x_ref, m_ref, g_ref, w_ref, b_ref, *rest, alpha):
    y = alpha * x_ref[...] + g_ref[...] * m_ref[...]
    x_new = _layer_norm_rows(y, w_ref[...], b_ref[...])
    if len(rest) == 1:
        rest[0][...] = x_new
    else:
        sc_ref, sh_ref, o_ref, h_ref = rest
        o_ref[...] = x_new
        h_ref[...] = (x_new * (1.0 + sc_ref[...]) + sh_ref[...]).astype(h_ref.dtype)


def ln_residual(x, m, g, w, b, alpha, tm, rows_per_seq, next_mod=None):
    n, d = x.shape
    row = pl.BlockSpec((tm, d), lambda i: (i, 0))
    vec = pl.BlockSpec((1, d), lambda i: (0, 0))
    in_specs = [row, row, _mod_spec(g, tm, rows_per_seq, 1), vec, vec]
    operands = [x, m, g, w.reshape(1, d), b.reshape(1, d)]
    out_shape = jax.ShapeDtypeStruct((n, d), F32)
    out_specs = row
    if next_mod is not None:
        in_specs += [_mod_spec(a, tm, rows_per_seq, 1) for a in next_mod]
        operands += list(next_mod)
        out_shape = (out_shape, jax.ShapeDtypeStruct((n, d), BF16))
        out_specs = (row, row)
    return pl.pallas_call(
        functools.partial(_ln_res_kernel, alpha=alpha),
        out_shape=out_shape,
        grid=(n // tm,),
        in_specs=in_specs,
        out_specs=out_specs,
        compiler_params=_cparams(("arbitrary",)),
        name="ln_residual",
    )(*operands)


def _modulate_kernel(x_ref, sc_ref, sh_ref, h_ref):
    h_ref[...] = (x_ref[...] * (1.0 + sc_ref[...]) + sh_ref[...]).astype(h_ref.dtype)


def modulate_cast(x, sc, sh, tm, rows_per_seq):
    n, d = x.shape
    row = pl.BlockSpec((tm, d), lambda i: (i, 0))
    return pl.pallas_call(
        _modulate_kernel,
        out_shape=jax.ShapeDtypeStruct((n, d), BF16),
        grid=(n // tm,),
        in_specs=[row, _mod_spec(sc, tm, rows_per_seq, 1), _mod_spec(sh, tm, rows_per_seq, 1)],
        out_specs=row,
        compiler_params=_cparams(("arbitrary",)),
        name="modulate_cast",
    )(x, sc, sh)


def _mm_ws_kernel(*refs, n_parts):
    x_refs = refs[:n_parts]
    w_ref, o_ref, wb_scr = refs[n_parts:]

    @pl.when(pl.program_id(1) == 0)
    def _():
        wb_scr[...] = w_ref[...].astype(BF16)

    k0 = 0
    acc = None
    for x_ref in x_refs:
        kw = x_ref.shape[1]
        part = _dot(x_ref[...], wb_scr[k0:k0 + kw, :])
        acc = part if acc is None else acc + part
        k0 += kw
    o_ref[...] = acc


def matmul_ws(x_parts, w, lead, n_cols, tn, tm, name):
    n = x_parts[0].shape[0]
    k = w.shape[-2]
    lead = tuple(lead)
    in_specs = [pl.BlockSpec((tm, xp.shape[1]), lambda j, i: (i, 0)) for xp in x_parts]
    in_specs.append(pl.BlockSpec((None,) * len(lead) + (k, tn), lambda j, i: lead + (0, j)))
    return pl.pallas_call(
        functools.partial(_mm_ws_kernel, n_parts=len(x_parts)),
        out_shape=jax.ShapeDtypeStruct((n, n_cols), F32),
        grid=(n_cols // tn, n // tm),
        in_specs=in_specs,
        out_specs=pl.BlockSpec((tm, tn), lambda j, i: (i, j)),
        scratch_shapes=[pltpu.VMEM((k, tn), BF16)],
        compiler_params=_cparams(("arbitrary", "arbitrary")),
        name=name,
    )(*x_parts, w)


def _ffn_up_ws_kernel(x_ref, wg_ref, wu_ref, a_ref, wgb_scr, wub_scr):
    @pl.when(pl.program_id(1) == 0)
    def _():
        wgb_scr[...] = wg_ref[...].astype(BF16)
        wub_scr[...] = wu_ref[...].astype(BF16)

    xb = x_ref[...]
    a_ref[...] = (_silu(_dot(xb, wgb_scr[...])) * _dot(xb, wub_scr[...])).astype(a_ref.dtype)


def ffn_up_ws(xb, w_gate, w_up, idx, tm, tf=512):
    n, d = xb.shape
    ff = w_gate.shape[2]
    wspec = pl.BlockSpec((None, d, tf), lambda j, i: (idx, 0, j))
    return pl.pallas_call(
        _ffn_up_ws_kernel,
        out_shape=jax.ShapeDtypeStruct((n, ff), BF16),
        grid=(pl.cdiv(ff, tf), n // tm),
        in_specs=[pl.BlockSpec((tm, d), lambda j, i: (i, 0)), wspec, wspec],
        out_specs=pl.BlockSpec((tm, tf), lambda j, i: (i, j)),
        scratch_shapes=[pltpu.VMEM((d, tf), BF16), pltpu.VMEM((d, tf), BF16)],
        compiler_params=_cparams(("arbitrary", "arbitrary")),
        name="ffn_up_ws",
    )(xb, w_gate, w_up)


def _ffn_up_kernel(x_ref, sc_ref, sh_ref, wg_ref, wu_ref, a_ref, xb_ref):
    @pl.when(pl.program_id(1) == 0)
    def _():
        h = x_ref[...] * (1.0 + sc_ref[...]) + sh_ref[...]
        xb_ref[...] = h.astype(BF16)

    xb = xb_ref[...]
    gate = _dot(xb, wg_ref[...].astype(BF16))
    up = _dot(xb, wu_ref[...].astype(BF16))
    a_ref[...] = (_silu(gate) * up).astype(a_ref.dtype)


def ffn_up(x, sc, sh, w_gate, w_up, idx, tm, rows_per_seq, tf=512):
    n, d = x.shape
    ff = w_gate.shape[2]
    return pl.pallas_call(
        _ffn_up_kernel,
        out_shape=jax.ShapeDtypeStruct((n, ff), BF16),
        grid=(n // tm, pl.cdiv(ff, tf)),
        in_specs=[
            pl.BlockSpec((tm, d), lambda i, j: (i, 0)),
            _mod_spec(sc, tm, rows_per_seq, 2),
            _mod_spec(sh, tm, rows_per_seq, 2),
            pl.BlockSpec((None, d, tf), lambda i, j: (idx, 0, j)),
            pl.BlockSpec((None, d, tf), lambda i, j: (idx, 0, j)),
        ],
        out_specs=pl.BlockSpec((tm, tf), lambda i, j: (i, j)),
        scratch_shapes=[pltpu.VMEM((tm, d), BF16)],
        compiler_params=_cparams(("arbitrary", "arbitrary")),
        name="ffn_up",
    )(x, sc, sh, w_gate, w_up)


def _ffn_down_kernel(a_ref, w_ref, o_ref):
    o_ref[...] = _dot(a_ref[...], w_ref[...].astype(BF16))


def ffn_down(a, w_down, idx, tm, tn=256):
    n, ff = a.shape
    d = w_down.shape[2]
    return pl.pallas_call(
        _ffn_down_kernel,
        out_shape=jax.ShapeDtypeStruct((n, d), F32),
        grid=(n // tm, d // tn),
        in_specs=[
            pl.BlockSpec((tm, ff), lambda i, j: (i, 0)),
            pl.BlockSpec((None, ff, tn), lambda i, j: (idx, 0, j)),
        ],
        out_specs=pl.BlockSpec((tm, tn), lambda i, j: (i, j)),
        compiler_params=_cparams(("arbitrary", "arbitrary")),
        name="ffn_down",
    )(a, w_down)


def _retention_kernel(q_ref, k_ref, v_ref, g_ref, cos_ref, sin_ref, dmat_ref, xi_ref, zeta_ref,
                      gc_ref, gnw_ref, r0_ref, y_ref, rn_ref, r_scr, *, n_chunks, n_heads):
    c = pl.program_id(1)

    @pl.when(c == 0)
    def _():
        r_scr[...] = r0_ref[...]

    cos = cos_ref[...]
    sin = sin_ref[...]

    def rope(x):
        return x * cos + pltpu.roll(x, HEAD_DIM // 2, axis=1) * sin

    for h in range(n_heads):
        hs = slice(h * HEAD_DIM, (h + 1) * HEAD_DIM)
        q = rope(q_ref[:, hs])
        k = rope(k_ref[:, hs]) * (HEAD_DIM ** -0.5)
        qb = q.astype(BF16)
        kb = k.astype(BF16)
        vb = v_ref[:, hs].astype(BF16)
        r_prev = r_scr[h]

        s = _dot_nt(qb, kb) * dmat_ref[h]
        y = _dot(s.astype(BF16), vb) + _dot(qb, r_prev.astype(BF16)) * xi_ref[h]
        kz_t = jnp.transpose(k * zeta_ref[h]).astype(BF16)
        r_scr[h] = gc_ref[h] * r_prev + _dot(kz_t, vb)

        mu = jnp.mean(y, axis=-1, keepdims=True)
        d = y - mu
        var = jnp.mean(d * d, axis=-1, keepdims=True)
        yn = d * lax.rsqrt(var + LN_EPS) * gnw_ref[:, hs]
        y_ref[:, hs] = (yn * _silu(g_ref[:, hs])).astype(y_ref.dtype)

    @pl.when(c == n_chunks - 1)
    def _():
        rn_ref[...] = r_scr[...]


def retention(z, tabs, gn_w, r0, n_heads, seq_len):
    n = z.shape[0]
    b = n // seq_len
    cch = RET_CHUNK
    nc = seq_len // cch
    hd = HEAD_DIM
    h_ = n_heads
    w = h_ * hd
    const3 = lambda bi, ci: (0, 0, 0)
    return pl.pallas_call(
        functools.partial(_retention_kernel, n_chunks=nc, n_heads=h_),
        out_shape=(jax.ShapeDtypeStruct((n, w), BF16), jax.ShapeDtypeStruct((b, h_, hd, hd), F32)),
        grid=(b, nc),
        in_specs=[
            pl.BlockSpec((cch, w), lambda bi, ci: (bi * nc + ci, 0)),
            pl.BlockSpec((cch, w), lambda bi, ci: (bi * nc + ci, 1)),
            pl.BlockSpec((cch, w), lambda bi, ci: (bi * nc + ci, 2)),
            pl.BlockSpec((cch, w), lambda bi, ci: (bi * nc + ci, 3)),
            pl.BlockSpec((cch, hd), lambda bi, ci: (ci, 0)),
            pl.BlockSpec((cch, hd), lambda bi, ci: (ci, 0)),
            pl.BlockSpec((h_, cch, cch), const3),
            pl.BlockSpec((h_, cch, hd), const3),
            pl.BlockSpec((h_, cch, hd), const3),
            pl.BlockSpec((h_, 1, hd), const3),
            pl.BlockSpec((1, w), lambda bi, ci: (0, 0)),
            pl.BlockSpec((None, h_, hd, hd), lambda bi, ci: (bi, 0, 0, 0)),
        ],
        out_specs=(pl.BlockSpec((cch, w), lambda bi, ci: (bi * nc + ci, 0)),
                   pl.BlockSpec((None, h_, hd, hd), lambda bi, ci: (bi, 0, 0, 0))),
        scratch_shapes=[pltpu.VMEM((h_, hd, hd), F32)],
        compiler_params=_cparams(("arbitrary", "arbitrary")),
        name="retention",
    )(z, z, z, z, tabs["cos"], tabs["sin"], tabs["dmat"], tabs["xi"], tabs["zeta"], tabs["gc"],
      gn_w.reshape(1, w), r0)


def retention_tables(n_heads, t_valid, pos0, t_pad):
    cch = min(t_valid, RET_CHUNK)
    half = HEAD_DIM // 2
    lg = jnp.log(1.0 - 2.0 ** (-5.0 - jnp.arange(n_heads, dtype=F32)))
    idx = jnp.arange(cch, dtype=F32)
    diff = idx[:, None] - idx[None, :]
    dmat = jnp.where(diff[None] >= 0, jnp.exp(jnp.maximum(diff, 0.0)[None] * lg[:, None, None]), 0.0)
    xi = jnp.exp((idx[None, :] + 1.0) * lg[:, None])
    zeta = jnp.exp((cch - 1.0 - idx)[None, :] * lg[:, None])
    gc = jnp.exp(cch * lg)
    pc = RET_CHUNK - cch
    dmat = jnp.pad(dmat, ((0, 0), (0, pc), (0, pc)))
    xi = jnp.broadcast_to(jnp.pad(xi, ((0, 0), (0, pc)))[:, :, None], (n_heads, RET_CHUNK, HEAD_DIM))
    zeta = jnp.broadcast_to(jnp.pad(zeta, ((0, 0), (0, pc)))[:, :, None], (n_heads, RET_CHUNK, HEAD_DIM))
    gc = jnp.broadcast_to(gc[:, None, None], (n_heads, 1, HEAD_DIM))
    pos = (pos0 + jnp.arange(t_pad)).astype(F32)
    inv = ROPE_BASE ** (-jnp.arange(half, dtype=F32) / half)
    ang = pos[:, None] * inv[None, :]
    cos = jnp.cos(ang)
    sin = jnp.sin(ang)
    return dict(dmat=dmat, xi=xi, zeta=zeta, gc=gc,
                cos=jnp.concatenate([cos, cos], axis=1), sin=jnp.concatenate([-sin, sin], axis=1))


CONV_ROWS = 32


def _conv_kernel(za_ref, zb_ref, buf_ref, cw_ref, cb_ref, lw_ref, lb_ref, o_ref, new_ref, u_scr,
                 *, tt, t_last, n_tiles):
    t = pl.program_id(1)

    @pl.when(t == 0)
    def _():
        u_scr[0:CONV_HALO, :] = buf_ref[...]

    @pl.when(t > 0)
    def _():
        u_scr[0:CONV_HALO, :] = u_scr[tt:tt + CONV_HALO, :]

    u_scr[CONV_HALO:CONV_HALO + tt, :] = za_ref[...] * jax.nn.sigmoid(zb_ref[...])

    off = CONV_HALO - (CONV_W - 1)
    for r0 in range(0, tt, CONV_ROWS):
        acc = u_scr[r0 + off:r0 + off + CONV_ROWS, :] * cw_ref[0:1, :]
        for w in range(1, CONV_W):
            acc = acc + u_scr[r0 + off + w:r0 + off + w + CONV_ROWS, :] * cw_ref[w:w + 1, :]
        y = _layer_norm_rows(acc + cb_ref[...], lw_ref[...], lb_ref[...])
        o_ref[r0:r0 + CONV_ROWS, :] = _silu(y).astype(o_ref.dtype)

    @pl.when(t == n_tiles - 1)
    def _():
        new_ref[...] = u_scr[t_last:t_last + CONV_HALO, :]


def conv_branch(z, col_a, col_b, conv_buf, cw, cb, lw, lb, seq_len, t_last, tt=128):
    n = z.shape[0]
    b = n // seq_len
    nt = seq_len // tt
    cdim = cw.shape[1]
    ja, jb = col_a // cdim, col_b // cdim
    return pl.pallas_call(
        functools.partial(_conv_kernel, tt=tt, t_last=t_last, n_tiles=nt),
        out_shape=(jax.ShapeDtypeStruct((n, cdim), BF16), jax.ShapeDtypeStruct((b, CONV_HALO, cdim), F32)),
        grid=(b, nt),
        in_specs=[
            pl.BlockSpec((tt, cdim), lambda bi, ti: (bi * nt + ti, ja)),
            pl.BlockSpec((tt, cdim), lambda bi, ti: (bi * nt + ti, jb)),
            pl.BlockSpec((None, CONV_HALO, cdim), lambda bi, ti: (bi, 0, 0)),
            pl.BlockSpec((CONV_HALO, cdim), lambda bi, ti: (0, 0)),
            pl.BlockSpec((1, cdim), lambda bi, ti: (0, 0)),
            pl.BlockSpec((1, cdim), lambda bi, ti: (0, 0)),
            pl.BlockSpec((1, cdim), lambda bi, ti: (0, 0)),
        ],
        out_specs=(pl.BlockSpec((tt, cdim), lambda bi, ti: (bi * nt + ti, 0)),
                   pl.BlockSpec((None, CONV_HALO, cdim), lambda bi, ti: (bi, 0, 0))),
        scratch_shapes=[pltpu.VMEM((CONV_HALO + tt, cdim), F32)],
        compiler_params=_cparams(("arbitrary", "arbitrary")),
        name="conv_branch",
    )(z, z, conv_buf, cw, cb.reshape(1, cdim), lw.reshape(1, cdim), lb.reshape(1, cdim))


def _tri_cumsum(tri_ref, x):
    hi, mid, lo = _split3(x)
    tri = tri_ref[...]
    return _dot(tri, hi) + _dot(tri, mid) + _dot(tri, lo)


LOG2E = 1.4426950408889634
N_BIAS_TERMS = 3


def _logf_kernel(zf_ref, bf_ref, tri_ref, lf_ref, c_ref, ct_ref, fq_ref, fk_ref, carry_ref, *, n_heads):
    @pl.when(pl.program_id(1) == 0)
    def _():
        carry_ref[...] = jnp.zeros_like(carry_ref)

    lf = _log_sigmoid(zf_ref[...] + bf_ref[...])
    c = _tri_cumsum(tri_ref, lf) + carry_ref[...]
    carry_ref[...] = c[LANES - 1:LANES, :]
    lf_ref[...] = lf
    c_ref[...] = c
    ct_ref[...] = jnp.transpose(c)[0:SUBLANES, :]

    lane = lax.broadcasted_iota(jnp.int32, c.shape, 1)
    ones_lanes = jnp.where(lane < 2 * N_BIAS_TERMS, 1.0, 0.0)
    for h in range(n_heads):
        col = jnp.sum(jnp.where(lane == h, c, 0.0), axis=1, keepdims=True) * LOG2E
        fq = ones_lanes
        fk = ones_lanes
        for i, t in enumerate(_split3(col)):
            t32 = t.astype(F32)
            fq = jnp.where(lane == i, t32, fq)
            fk = jnp.where(lane == N_BIAS_TERMS + i, -t32, fk)
        fq_ref[:, h * LANES:(h + 1) * LANES] = fq.astype(BF16)
        fk_ref[:, h * LANES:(h + 1) * LANES] = fk.astype(BF16)


def forget_cumsum(zf, b_forget_row, tri, seq_len, n_heads):
    n = zf.shape[0]
    b = n // seq_len
    nt = seq_len // LANES
    w = n_heads * LANES
    return pl.pallas_call(
        functools.partial(_logf_kernel, n_heads=n_heads),
        out_shape=(jax.ShapeDtypeStruct((n, LANES), F32), jax.ShapeDtypeStruct((n, LANES), F32),
                   jax.ShapeDtypeStruct((b, SUBLANES, seq_len), F32),
                   jax.ShapeDtypeStruct((n, w), BF16), jax.ShapeDtypeStruct((n, w), BF16)),
        grid=(b, nt),
        in_specs=[
            pl.BlockSpec((LANES, LANES), lambda bi, ti: (bi * nt + ti, 0)),
            pl.BlockSpec((1, LANES), lambda bi, ti: (0, 0)),
            pl.BlockSpec((LANES, LANES), lambda bi, ti: (0, 0)),
        ],
        out_specs=(pl.BlockSpec((LANES, LANES), lambda bi, ti: (bi * nt + ti, 0)),
                   pl.BlockSpec((LANES, LANES), lambda bi, ti: (bi * nt + ti, 0)),
                   pl.BlockSpec((None, SUBLANES, LANES), lambda bi, ti: (bi, 0, ti)),
                   pl.BlockSpec((LANES, w), lambda bi, ti: (bi * nt + ti, 0)),
                   pl.BlockSpec((LANES, w), lambda bi, ti: (bi * nt + ti, 0))),
        scratch_shapes=[pltpu.VMEM((1, LANES), F32)],
        compiler_params=_cparams(("arbitrary", "arbitrary")),
        name="forget_cumsum",
    )(zf, b_forget_row, tri)


FOX_COL_BLOCK = 2 * HEAD_DIM


def _fox_kernel(qi_ref, ki_ref, *refs, tq, tk, n_heads):
    nb = n_heads * HEAD_DIM // FOX_COL_BLOCK
    q_refs, k_refs, v_refs = refs[0:nb], refs[nb:2 * nb], refs[2 * nb:3 * nb]
    fq_ref, fk_ref, o_ref, m_scr, l_scr, acc_scr, qa_scr = refs[3 * nb:]
    step = pl.program_id(1)
    qi = qi_ref[step]
    ki = ki_ref[step]

    def head_cols(block_refs, h):
        lo = (h * HEAD_DIM) % FOX_COL_BLOCK
        return block_refs[h * HEAD_DIM // FOX_COL_BLOCK][:, lo:lo + HEAD_DIM]

    @pl.when(ki == 0)
    def _():
        m_scr[...] = jnp.full_like(m_scr, NEG_INF)
        l_scr[...] = jnp.zeros_like(l_scr)
        acc_scr[...] = jnp.zeros_like(acc_scr)
        for h in range(n_heads):
            qa_scr[h, :, 0:HEAD_DIM] = (head_cols(q_refs, h) * (HEAD_DIM ** -0.5 * LOG2E)).astype(BF16)
            qa_scr[h, :, HEAD_DIM:] = fq_ref[:, h * LANES:(h + 1) * LANES]

    def update(h, masked):
        ka = jnp.concatenate([head_cols(k_refs, h).astype(BF16), fk_ref[:, h * LANES:(h + 1) * LANES]], axis=1)
        vb = head_cols(v_refs, h).astype(BF16)
        s = _dot_nt(qa_scr[h], ka)
        if masked:
            row = lax.broadcasted_iota(jnp.int32, (tq, tk), 0)
            col = lax.broadcasted_iota(jnp.int32, (tq, tk), 1)
            s = jnp.where(col <= row, s, NEG_INF)
        m_prev = m_scr[h]
        m_new = jnp.maximum(m_prev, jnp.max(s, axis=1, keepdims=True))
        alpha = jnp.exp2(m_prev - m_new)
        p = jnp.exp2(s - m_new[:, 0:1])
        l_scr[h] = alpha * l_scr[h] + jnp.sum(p, axis=1, keepdims=True)
        acc_scr[h] = alpha * acc_scr[h] + _dot(p.astype(BF16), vb)
        m_scr[h] = m_new

    @pl.when(ki < qi)
    def _():
        for h in range(n_heads):
            update(h, False)

    @pl.when(ki == qi)
    def _():
        for h in range(n_heads):
            update(h, True)
            o_ref[:, h * HEAD_DIM:(h + 1) * HEAD_DIM] = (acc_scr[h] / l_scr[h]).astype(o_ref.dtype)


def fox_attention(z, col_q, col_k, col_v, fq, fk, n_heads, seq_len, tq=512):
    n = z.shape[0]
    b = n // seq_len
    tk = tq
    nq = seq_len // tq
    hd = HEAD_DIM
    w = n_heads * hd
    nb = w // FOX_COL_BLOCK
    pairs = [(qi, ki) for qi in range(nq) for ki in range(qi + 1)]
    qi_tab = jnp.asarray([p[0] for p in pairs], jnp.int32)
    ki_tab = jnp.asarray([p[1] for p in pairs], jnp.int32)

    def cols(col0, use_q):
        j0 = col0 // FOX_COL_BLOCK
        tile = tq if use_q else tk
        if use_q:
            return [pl.BlockSpec((tile, FOX_COL_BLOCK), lambda bi, s, qt, kt, j=j0 + i: (bi * nq + qt[s], j))
                    for i in range(nb)]
        return [pl.BlockSpec((tile, FOX_COL_BLOCK), lambda bi, s, qt, kt, j=j0 + i: (bi * nq + kt[s], j))
                for i in range(nb)]

    grid_spec = pltpu.PrefetchScalarGridSpec(
        num_scalar_prefetch=2,
        grid=(b, len(pairs)),
        in_specs=cols(col_q, True) + cols(col_k, False) + cols(col_v, False) + [
            pl.BlockSpec((tq, w), lambda bi, s, qt, kt: (bi * nq + qt[s], 0)),
            pl.BlockSpec((tk, w), lambda bi, s, qt, kt: (bi * nq + kt[s], 0)),
        ],
        out_specs=pl.BlockSpec((tq, w), lambda bi, s, qt, kt: (bi * nq + qt[s], 0)),
        scratch_shapes=[pltpu.VMEM((n_heads, tq, LANES), F32), pltpu.VMEM((n_heads, tq, LANES), F32),
                        pltpu.VMEM((n_heads, tq, hd), F32), pltpu.VMEM((n_heads, tq, hd + LANES), BF16)],
    )
    return pl.pallas_call(
        functools.partial(_fox_kernel, tq=tq, tk=tk, n_heads=n_heads),
        out_shape=jax.ShapeDtypeStruct((n, w), BF16),
        grid_spec=grid_spec,
        compiler_params=_cparams(("arbitrary", "arbitrary")),
        name="fox_attention",
    )(qi_tab, ki_tab, *([z] * (3 * nb)), fq, fk)


def _fox_paged_kernel(pt_ref, *refs, n_group, n_heads, t_pad, n_steps):
    del pt_ref
    g_ = n_group
    k_refs = refs[0:g_]
    v_refs = refs[g_:2 * g_]
    lf_refs = refs[2 * g_:3 * g_]
    (q_ref, kn_ref, vn_ref, cn_ref, ctn_ref, tri_ref, ones_ref, o_ref,
     m_scr, l_scr, acc_scr, carry_scr) = refs[3 * g_:]
    g = pl.program_id(1)
    scale = HEAD_DIM ** -0.5

    @pl.when(g == 0)
    def _():
        m_scr[...] = jnp.full_like(m_scr, NEG_INF)
        l_scr[...] = jnp.zeros_like(l_scr)
        acc_scr[...] = jnp.zeros_like(acc_scr)
        carry_scr[...] = jnp.zeros_like(carry_scr)

    lf_all = jnp.concatenate([r[...] for r in lf_refs], axis=0)
    hi, mid, lo = _split3(lf_all)
    tri = tri_ref[...]
    ones = ones_ref[...]
    after = _dot(hi, tri) + _dot(mid, tri) + _dot(lo, tri)
    total = _dot(hi, ones) + _dot(mid, ones) + _dot(lo, ones)
    carry = carry_scr[...]
    page_bias = []
    for i in range(g_):
        page_bias.append(carry + after[i * SUBLANES:(i + 1) * SUBLANES, :])
        carry = carry + total[i * SUBLANES:(i + 1) * SUBLANES, :]
    carry_scr[...] = carry

    cn = cn_ref[...]
    lane = lax.broadcasted_iota(jnp.int32, cn.shape, 1)

    def flash_update(h, s, v_list):
        m_prev = m_scr[h]
        m_new = jnp.maximum(m_prev, jnp.max(s, axis=1, keepdims=True))
        alpha = jnp.exp(m_prev - m_new)
        p32 = jnp.exp(s - m_new[:, 0:1])
        l_scr[h] = alpha * l_scr[h] + jnp.sum(p32, axis=1, keepdims=True)
        p = p32.astype(BF16)
        pv = _dot(p[:, 0:LANES], v_list[0])
        for i in range(1, len(v_list)):
            pv = pv + _dot(p[:, i * LANES:(i + 1) * LANES], v_list[i])
        acc_scr[h] = alpha * acc_scr[h] + pv
        m_scr[h] = m_new

    def scores(h):
        qh = q_ref[:, h * HEAD_DIM:(h + 1) * HEAD_DIM].astype(BF16)
        cq = jnp.sum(jnp.where(lane == h, cn, 0.0), axis=1, keepdims=True)
        s_list = []
        for i in range(g_):
            kh = k_refs[i][h].astype(BF16)
            s_list.append(_dot_nt(qh, kh) * scale + (cq + page_bias[i][h:h + 1, :]))
        return jnp.concatenate(s_list, axis=1)

    s_all = [scores(h) for h in range(n_heads)]
    for h in range(n_heads):
        flash_update(h, s_all[h], [v_refs[i][h].astype(BF16) for i in range(g_)])

    @pl.when(g == n_steps - 1)
    def _():
        row = lax.broadcasted_iota(jnp.int32, (t_pad, LANES), 0)
        col = lax.broadcasted_iota(jnp.int32, (t_pad, LANES), 1)
        for h in range(n_heads):
            hs = slice(h * HEAD_DIM, (h + 1) * HEAD_DIM)
            qh = q_ref[:, hs].astype(BF16)
            cq = jnp.sum(jnp.where(lane == h, cn, 0.0), axis=1, keepdims=True)
            s = _dot_nt(qh, kn_ref[:, hs].astype(BF16)) * scale + (cq - ctn_ref[h:h + 1, :])
            s = jnp.where(col <= row, s, NEG_INF)
            flash_update(h, s, [vn_ref[:, hs].astype(BF16)])
            o_ref[:, hs] = acc_scr[h] / l_scr[h]


def fox_paged_attention(q, k_new, v_new, c_new, ct_new, cache_k, cache_v, logf_t, page_table, layer,
                        tri_after, ones_mat, n_group=16, t_pad=SUBLANES):
    b, n_pages = page_table.shape
    n_heads = cache_k.shape[2]
    page = cache_k.shape[3]
    hd = HEAD_DIM
    w = n_heads * hd
    ng = n_pages // n_group

    def page_of(i):
        return lambda bi, gi, pt: (pt[bi, n_pages - 1 - (gi * n_group + i)], layer, 0, 0, 0)

    def lf_of(i):
        return lambda bi, gi, pt: (layer, pt[bi, n_pages - 1 - (gi * n_group + i)], 0, 0)

    kv_specs = [pl.BlockSpec((None, None, n_heads, page, hd), page_of(i)) for i in range(n_group)]
    lf_specs = [pl.BlockSpec((None, None, SUBLANES, page), lf_of(i)) for i in range(n_group)]
    grid_spec = pltpu.PrefetchScalarGridSpec(
        num_scalar_prefetch=1,
        grid=(b, ng),
        in_specs=kv_specs + kv_specs + lf_specs + [
            pl.BlockSpec((None, t_pad, w), lambda bi, gi, pt: (bi, 0, 0)),
            pl.BlockSpec((None, LANES, w), lambda bi, gi, pt: (bi, 0, 0)),
            pl.BlockSpec((None, LANES, w), lambda bi, gi, pt: (bi, 0, 0)),
            pl.BlockSpec((t_pad, LANES), lambda bi, gi, pt: (bi * (LANES // t_pad), 0)),
            pl.BlockSpec((None, SUBLANES, LANES), lambda bi, gi, pt: (bi, 0, 0)),
            pl.BlockSpec((LANES, LANES), lambda bi, gi, pt: (0, 0)),
            pl.BlockSpec((LANES, LANES), lambda bi, gi, pt: (0, 0)),
        ],
        out_specs=pl.BlockSpec((None, t_pad, w), lambda bi, gi, pt: (bi, 0, 0)),
        scratch_shapes=[pltpu.VMEM((n_heads, t_pad, LANES), F32), pltpu.VMEM((n_heads, t_pad, LANES), F32),
                        pltpu.VMEM((n_heads, t_pad, hd), F32), pltpu.VMEM((SUBLANES, LANES), F32)],
    )
    return pl.pallas_call(
        functools.partial(_fox_paged_kernel, n_group=n_group, n_heads=n_heads, t_pad=t_pad, n_steps=ng),
        out_shape=jax.ShapeDtypeStruct((b, t_pad, w), F32),
        grid_spec=grid_spec,
        compiler_params=_cparams(("arbitrary", "arbitrary")),
        name="fox_paged_attention",
    )(page_table, *([cache_k] * n_group), *([cache_v] * n_group), *([logf_t] * n_group),
      q, k_new, v_new, c_new, ct_new, tri_after, ones_mat)


def _router_kernel(x_ref, sc_ref, sh_ref, wr_ref, br_ref, h_ref, info_ref, *, n_experts):
    h = x_ref[...] * (1.0 + sc_ref[...]) + sh_ref[...]
    h_ref[...] = h
    logits = _dot(h.astype(BF16), wr_ref[...].astype(BF16)) + br_ref[...]
    lane = lax.broadcasted_iota(jnp.int32, logits.shape, 1)
    neg = -jnp.inf
    l1 = jnp.where(lane < n_experts, logits, neg)
    m1 = jnp.max(l1, axis=1, keepdims=True)
    i1 = jnp.min(jnp.where(l1 == m1, lane, LANES), axis=1, keepdims=True)
    l2 = jnp.where(lane == i1, neg, l1)
    m2 = jnp.max(l2, axis=1, keepdims=True)
    i2 = jnp.min(jnp.where(l2 == m2, lane, LANES), axis=1, keepdims=True)
    e = jnp.exp(m2 - m1)
    den = 1.0 + e
    p1 = 1.0 / den
    p2 = e / den
    info = jnp.where(lane == 0, i1.astype(F32),
                     jnp.where(lane == 1, i2.astype(F32),
                               jnp.where(lane == 2, p1, jnp.where(lane == 3, p2, 0.0))))
    info_ref[...] = info


def moe_route(x, sc, sh, w_router_pad, b_router_pad, n_experts, tm, rows_per_seq):
    n, d = x.shape
    return pl.pallas_call(
        functools.partial(_router_kernel, n_experts=n_experts),
        out_shape=(jax.ShapeDtypeStruct((n, d), F32), jax.ShapeDtypeStruct((n, LANES), F32)),
        grid=(n // tm,),
        in_specs=[
            pl.BlockSpec((tm, d), lambda i: (i, 0)),
            _mod_spec(sc, tm, rows_per_seq, 1),
            _mod_spec(sh, tm, rows_per_seq, 1),
            pl.BlockSpec((d, LANES), lambda i: (0, 0)),
            pl.BlockSpec((1, LANES), lambda i: (0, 0)),
        ],
        out_specs=(pl.BlockSpec((tm, d), lambda i: (i, 0)), pl.BlockSpec((tm, LANES), lambda i: (i, 0))),
        compiler_params=_cparams(("arbitrary",)),
        name="moe_router",
    )(x, sc, sh, w_router_pad, b_router_pad)


def moe_plan(experts, n_experts, tm):
    n = experts.shape[0]
    na = n * TOP_K
    n_tiles = pl.cdiv(na, tm) + n_experts
    flat_e = experts.reshape(na)
    onehot = (flat_e[:, None] == jnp.arange(n_experts, dtype=jnp.int32)[None, :]).astype(jnp.int32)
    incl = jnp.cumsum(onehot, axis=0)
    rank = jnp.sum((incl - onehot) * onehot, axis=1)
    counts = incl[-1]
    tiles_e = (counts + tm - 1) // tm
    tile_end = jnp.cumsum(tiles_e)
    tile_off = tile_end - tiles_e
    n_used = tile_end[-1:]
    dest = (jnp.sum(tile_off[None, :] * onehot, axis=1) * tm + rank).astype(jnp.int32)
    src_token = jnp.zeros((n_tiles * tm,), jnp.int32).at[dest].set(jnp.arange(na, dtype=jnp.int32) // TOP_K)
    t = jnp.arange(n_tiles, dtype=jnp.int32)
    tile_expert = jnp.sum((t[:, None] >= tile_end[None, :]).astype(jnp.int32), axis=1)
    tile_expert = jnp.minimum(tile_expert, n_experts - 1).astype(jnp.int32)
    return dest, src_token, tile_expert, n_used.astype(jnp.int32)


def _row_copy(src_ref, row, dst_ref, slot, sem):
    return pltpu.make_async_copy(src_ref.at[pl.ds(row, 1), :], dst_ref.at[pl.ds(slot, 1), :], sem)


GATHER_UNROLL = 8


def _moe_gather_kernel(src_ref, nused_ref, h_ref, o_ref, buf, sem, *, tm):
    t = pl.program_id(0)

    @pl.when(t < nused_ref[0])
    def _():
        base = t * tm

        def issue(r8, carry):
            for u in range(GATHER_UNROLL):
                r = r8 * GATHER_UNROLL + u
                _row_copy(h_ref, src_ref[base + r], buf, r, sem).start(priority=u % 2)
            return carry

        lax.fori_loop(0, tm // GATHER_UNROLL, issue, 0)

        def drain(r, carry):
            _row_copy(h_ref, 0, buf, r, sem).wait()
            return carry

        lax.fori_loop(0, tm, drain, 0, unroll=GATHER_UNROLL)
        o_ref[...] = buf[...].astype(BF16)

    @pl.when(t >= nused_ref[0])
    def _():
        o_ref[...] = jnp.zeros_like(o_ref)


def moe_gather(h, src_token, n_used, tm):
    s = src_token.shape[0]
    d = h.shape[1]
    n_tiles = s // tm
    grid_spec = pltpu.PrefetchScalarGridSpec(
        num_scalar_prefetch=2,
        grid=(n_tiles,),
        in_specs=[pl.BlockSpec(memory_space=pl.ANY)],
        out_specs=pl.BlockSpec((tm, d), lambda t, src, nu: (t, 0)),
        scratch_shapes=[pltpu.VMEM((tm, d), F32), pltpu.SemaphoreType.DMA(())],
    )
    return pl.pallas_call(
        functools.partial(_moe_gather_kernel, tm=tm),
        out_shape=jax.ShapeDtypeStruct((s, d), BF16),
        grid_spec=grid_spec,
        compiler_params=_cparams(("arbitrary",)),
        name="moe_gather",
    )(src_token, n_used, h)


def _moe_up_kernel(te_ref, nused_ref, x_ref, wg_ref, wu_ref, a_ref):
    del te_ref
    t = pl.program_id(1)

    @pl.when(t < nused_ref[0])
    def _():
        xb = x_ref[...]
        gate = _dot(xb, wg_ref[...].astype(BF16))
        up = _dot(xb, wu_ref[...].astype(BF16))
        a_ref[...] = (_silu(gate) * up).astype(a_ref.dtype)

    @pl.when(t >= nused_ref[0])
    def _():
        a_ref[...] = jnp.zeros_like(a_ref)


def moe_up(xs, w_gate, w_up, idx, tile_expert, n_used, tm, tf=512):
    s, d = xs.shape
    ff = w_gate.shape[3]
    n_tiles = s // tm

    def tile(t, nu):
        return jnp.minimum(t, nu[0] - 1)

    grid_spec = pltpu.PrefetchScalarGridSpec(
        num_scalar_prefetch=2,
        grid=(pl.cdiv(ff, tf), n_tiles),
        in_specs=[
            pl.BlockSpec((tm, d), lambda f, t, te, nu: (tile(t, nu), 0)),
            pl.BlockSpec((None, None, d, tf), lambda f, t, te, nu: (idx, te[tile(t, nu)], 0, f)),
            pl.BlockSpec((None, None, d, tf), lambda f, t, te, nu: (idx, te[tile(t, nu)], 0, f)),
        ],
        out_specs=pl.BlockSpec((tm, tf), lambda f, t, te, nu: (t, f)),
    )
    return pl.pallas_call(
        _moe_up_kernel,
        out_shape=jax.ShapeDtypeStruct((s, ff), BF16),
        grid_spec=grid_spec,
        compiler_params=_cparams(("arbitrary", "arbitrary")),
        name="moe_up",
    )(tile_expert, n_used, xs, w_gate, w_up)


def _moe_down_kernel(te_ref, nused_ref, a_ref, w_ref, o_ref):
    del te_ref
    t = pl.program_id(1)

    @pl.when(t < nused_ref[0])
    def _():
        o_ref[...] = _dot(a_ref[...], w_ref[...].astype(BF16))

    @pl.when(t >= nused_ref[0])
    def _():
        o_ref[...] = jnp.zeros_like(o_ref)


def moe_down(a, w_down, idx, tile_expert, n_used, tm, tn=512):
    s, ff = a.shape
    d = w_down.shape[3]
    n_tiles = s // tm

    def tile(t, nu):
        return jnp.minimum(t, nu[0] - 1)

    grid_spec = pltpu.PrefetchScalarGridSpec(
        num_scalar_prefetch=2,
        grid=(d // tn, n_tiles),
        in_specs=[
            pl.BlockSpec((tm, ff), lambda j, t, te, nu: (tile(t, nu), 0)),
            pl.BlockSpec((None, None, ff, tn), lambda j, t, te, nu: (idx, te[tile(t, nu)], 0, j)),
        ],
        out_specs=pl.BlockSpec((tm, tn), lambda j, t, te, nu: (t, j)),
    )
    return pl.pallas_call(
        _moe_down_kernel,
        out_shape=jax.ShapeDtypeStruct((s, d), F32),
        grid_spec=grid_spec,
        compiler_params=_cparams(("arbitrary", "arbitrary")),
        name="moe_down",
    )(tile_expert, n_used, a, w_down)


def _moe_combine_kernel(dest_ref, y_ref, info_ref, x_ref, g_ref, w_ref, b_ref, o_ref, buf0, buf1, sem,
                        *, tm, row0, alpha):
    base = (row0 + pl.program_id(0) * tm) * TOP_K

    def issue(r8, carry):
        for u in range(GATHER_UNROLL):
            r = r8 * GATHER_UNROLL + u
            _row_copy(y_ref, dest_ref[base + TOP_K * r], buf0, r, sem).start(priority=0)
            _row_copy(y_ref, dest_ref[base + TOP_K * r + 1], buf1, r, sem).start(priority=1)
        return carry

    lax.fori_loop(0, tm // GATHER_UNROLL, issue, 0)

    def drain(r, carry):
        _row_copy(y_ref, 0, buf0, r, sem).wait()
        _row_copy(y_ref, 0, buf1, r, sem).wait()
        return carry

    lax.fori_loop(0, tm, drain, 0, unroll=GATHER_UNROLL)
    info = info_ref[...]
    f = info[:, 2:3] * buf0[...] + info[:, 3:4] * buf1[...]
    y = alpha * x_ref[...] + g_ref[...] * f
    o_ref[...] = _layer_norm_rows(y, w_ref[...], b_ref[...])


def moe_combine_ln(y_sorted, dest, info, x, g, w, b, row0, alpha, tm, rows_per_seq):
    n, d = x.shape
    gspec = _mod_spec(g, tm, rows_per_seq, 1)
    gmap = gspec.index_map
    grid_spec = pltpu.PrefetchScalarGridSpec(
        num_scalar_prefetch=1,
        grid=(n // tm,),
        in_specs=[
            pl.BlockSpec(memory_space=pl.ANY),
            pl.BlockSpec((tm, LANES), lambda i, dst: (i, 0)),
            pl.BlockSpec((tm, d), lambda i, dst: (i, 0)),
            pl.BlockSpec(gspec.block_shape, lambda i, dst: gmap(i)),
            pl.BlockSpec((1, d), lambda i, dst: (0, 0)),
            pl.BlockSpec((1, d), lambda i, dst: (0, 0)),
        ],
        out_specs=pl.BlockSpec((tm, d), lambda i, dst: (i, 0)),
        scratch_shapes=[pltpu.VMEM((tm, d), F32), pltpu.VMEM((tm, d), F32), pltpu.SemaphoreType.DMA(())],
    )
    return pl.pallas_call(
        functools.partial(_moe_combine_kernel, tm=tm, row0=row0, alpha=alpha),
        out_shape=jax.ShapeDtypeStruct((n, d), F32),
        grid_spec=grid_spec,
        compiler_params=_cparams(("arbitrary",)),
        name="moe_combine_ln",
    )(dest, y_sorted, info, x, g, w.reshape(1, d), b.reshape(1, d))


def _kv_pack_kernel(*refs, depth, n_heads):
    nb = n_heads * HEAD_DIM // FOX_COL_BLOCK
    k_out, v_out = refs[2 * depth * nb:]
    for l in range(depth):
        for which, out in ((0, k_out), (1, v_out)):
            blocks = refs[(2 * l + which) * nb:(2 * l + which + 1) * nb]
            for h in range(n_heads):
                lo = (h * HEAD_DIM) % FOX_COL_BLOCK
                out[l, h] = blocks[h * HEAD_DIM // FOX_COL_BLOCK][:, lo:lo + HEAD_DIM]


def kv_pack(z_layers, col_k, col_v, n_heads, seq_len, tt=256):
    depth = len(z_layers)
    n = z_layers[0].shape[0]
    b = n // seq_len
    nt = seq_len // tt
    nb = n_heads * HEAD_DIM // FOX_COL_BLOCK

    def cols(col0):
        j0 = col0 // FOX_COL_BLOCK
        return [pl.BlockSpec((tt, FOX_COL_BLOCK), lambda bi, ti, j=j0 + i: (bi * nt + ti, j)) for i in range(nb)]

    in_specs, operands = [], []
    for z in z_layers:
        in_specs += cols(col_k) + cols(col_v)
        operands += [z] * (2 * nb)
    out_sds = jax.ShapeDtypeStruct((b, depth, n_heads, seq_len, HEAD_DIM), F32)
    out_spec = pl.BlockSpec((None, depth, n_heads, tt, HEAD_DIM), lambda bi, ti: (bi, 0, 0, ti, 0))
    return pl.pallas_call(
        functools.partial(_kv_pack_kernel, depth=depth, n_heads=n_heads),
        out_shape=(out_sds, out_sds),
        grid=(b, nt),
        in_specs=in_specs,
        out_specs=(out_spec, out_spec),
        compiler_params=_cparams(("arbitrary", "arbitrary")),
        name="kv_pack",
    )(*operands)


PROMPT_TM = 1024
PROMPT_WS_TM = 2048
PROJ_IN_TN = 1280
PROJ_OUT_TN = 512
PROMPT_LN_TM = 512
MOE_TM = 512
MOE_COMBINE_TM = 256


def _pad_axis(a, axis, size):
    pad = [(0, 0)] * a.ndim
    pad[axis] = (0, size - a.shape[axis])
    return jnp.pad(a, pad)


def kernel(x_prompt, x_sample, cache_k, cache_v, cache_logf, state_ret, state_conv, page_table, c_prompt, c_sample, w_ada, b_ada, w_in, b_forget, ret_gn_w, conv_w, conv_b, conv_ln_w, conv_ln_b, w_out, ln1_w, ln1_b, ln2_w, ln2_b, ffn_w_gate, ffn_w_up, ffn_w_down, moe_router, moe_router_b, moe_w_gate, moe_w_up, moe_w_down):
    bp, tp, d = x_prompt.shape
    bs, ts, _ = x_sample.shape
    depth = w_in.shape[0]
    ret_heads = state_ret.shape[2]
    fox_heads = cache_k.shape[3]
    page = cache_k.shape[2]
    conv_dim = state_conv.shape[3]
    n_experts = moe_router.shape[2]
    hd = HEAD_DIM
    ret_w, fox_w = ret_heads * hd, fox_heads * hd
    o_ca = 4 * ret_w
    o_cb = o_ca + conv_dim
    o_fq = o_cb + conv_dim
    o_fk = o_fq + fox_w
    o_fv = o_fk + fox_w
    n_main = o_fv + fox_w
    alpha = (2 * depth) ** 0.25
    past_len = page_table.shape[1] * page
    tpad = RET_CHUNK
    np_rows, ns_rows = bp * tp, bs * ts

    w_forget = _pad_axis(w_in[:, :, n_main:], 2, LANES)
    bf_rows = _pad_axis(b_forget, 1, LANES).reshape(depth, 1, LANES)
    router_w = _pad_axis(moe_router, 2, LANES)
    router_b = _pad_axis(moe_router_b, 1, LANES).reshape(-1, 1, LANES)
    conv_w_pad = _pad_axis(conv_w, 1, CONV_HALO)
    logf_t = _pad_axis(jnp.transpose(cache_logf, (1, 0, 3, 2)), 2, SUBLANES)
    cache_k_hm = jnp.transpose(cache_k, (0, 1, 3, 2, 4))
    cache_v_hm = jnp.transpose(cache_v, (0, 1, 3, 2, 4))
    tri_incl = jnp.tril(jnp.ones((LANES, LANES), BF16))
    tri_after = jnp.tril(jnp.ones((page, page), BF16), -1)
    ones_mat = jnp.ones((page, page), BF16)
    tabs_p = retention_tables(ret_heads, tp, 0, tp)
    tabs_s = retention_tables(ret_heads, ts, past_len, tpad)
    r0_p = jnp.zeros((bp, ret_heads, hd, hd), F32)
    buf0_p = jnp.zeros((bp, CONV_HALO, conv_dim), F32)

    c_rows = _pad_axis(jnp.concatenate([c_prompt, c_sample], axis=0), 0, 2 * SUBLANES)
    mod = ada_modulation(c_rows, w_ada, b_ada)

    xp = x_prompt.reshape(np_rows, d)
    xs = x_sample.reshape(ns_rows, d)
    rets_p, rets_s, convs_p, convs_s, zs_p, ks_s, vs_s, lfs_p, lfs_s = ([] for _ in range(9))

    def unpad(a):
        return a.reshape(bs, tpad, -1)[:, :ts].reshape(ns_rows, -1)

    def prompt_mod(l):
        return [mod[l, :bp, i * d:(i + 1) * d].reshape(bp, 1, d) for i in range(6)]

    sh1p, sc1p = prompt_mod(0)[:2]
    hb = modulate_cast(xp, sc1p, sh1p, PROMPT_LN_TM, tp)

    for l in range(depth):
        parts = [mod[l, :, i * d:(i + 1) * d] for i in range(6)]
        sh1p, sc1p, g1p, sh2p, sc2p, g2p = prompt_mod(l)
        sh1s, sc1s, g1s, sh2s, sc2s, g2s = (jnp.repeat(p[bp:bp + bs], ts, axis=0) for p in parts)
        next_in = prompt_mod(l + 1)[1::-1] if l + 1 < depth else None

        z = matmul_ws([hb], w_in, (l,), n_main, PROJ_IN_TN, PROMPT_TM, "proj_in_ws")
        zf = matmul_ws([hb], w_forget, (l,), LANES, LANES, PROMPT_TM, "proj_forget_ws")
        ret_o, r_new = retention(z, tabs_p, ret_gn_w[l], r0_p, ret_heads, tp)
        conv_o, conv_new = conv_branch(z, o_ca, o_cb, buf0_p, conv_w_pad[l], conv_b[l], conv_ln_w[l],
                                       conv_ln_b[l], tp, RET_CHUNK)
        lf, _, _, bias_q, bias_k = forget_cumsum(zf, bf_rows[l], tri_incl, tp, fox_heads)
        fox_o = fox_attention(z, o_fq, o_fk, o_fv, bias_q, bias_k, fox_heads, tp)
        m = matmul_ws([ret_o, conv_o, fox_o], w_out, (l,), d, PROJ_OUT_TN, PROMPT_WS_TM, "proj_out_ws")
        if l % 2 == 0:
            x1p, hb2 = ln_residual(xp, m, g1p, ln1_w[l], ln1_b[l], alpha, PROMPT_LN_TM, tp, next_mod=(sc2p, sh2p))
        else:
            x1p = ln_residual(xp, m, g1p, ln1_w[l], ln1_b[l], alpha, PROMPT_LN_TM, tp)
        rets_p.append(r_new)
        convs_p.append(conv_new[:, CONV_HALO - (CONV_W - 1):])
        zs_p.append(z)
        lfs_p.append(lf[:, :fox_heads].reshape(bp, tp, fox_heads))

        zs, zfs = proj_in(xs, sc1s, sh1s, w_in, w_forget, l, n_main, ns_rows, ts, tn=640)
        zs_pad = _pad_axis(zs.reshape(bs, ts, n_main), 1, tpad)
        z2 = zs_pad.reshape(bs * tpad, n_main)
        zf2 = _pad_axis(zfs.reshape(bs, ts, LANES), 1, tpad).reshape(bs * tpad, LANES)
        ret_s, r_new_s = retention(z2, tabs_s, ret_gn_w[l], state_ret[:, l], ret_heads, tpad)
        buf_s = jnp.pad(state_conv[:, l], ((0, 0), (CONV_HALO - (CONV_W - 1), 0), (0, 0)))
        conv_s, conv_new_s = conv_branch(z2, o_ca, o_cb, buf_s, conv_w_pad[l], conv_b[l], conv_ln_w[l],
                                         conv_ln_b[l], tpad, ts)
        lf_s, csum_s, csum_t_s, _, _ = forget_cumsum(zf2, bf_rows[l], tri_incl, tpad, fox_heads)
        fox_s = fox_paged_attention(zs_pad[:, :SUBLANES, o_fq:o_fk], zs_pad[:, :, o_fk:o_fv],
                                    zs_pad[:, :, o_fv:n_main], csum_s, csum_t_s, cache_k_hm, cache_v_hm, logf_t,
                                    page_table, l, tri_after, ones_mat)
        fox_s = fox_s[:, :ts].reshape(ns_rows, fox_w)
        m_s = proj_out(unpad(ret_s), unpad(conv_s), fox_s, w_out, l, ns_rows)
        x1s = ln_residual(xs, m_s, g1s, ln1_w[l], ln1_b[l], alpha, ns_rows, ts)
        rets_s.append(r_new_s)
        convs_s.append(conv_new_s[:, CONV_HALO - (CONV_W - 1):])
        ks_s.append(zs[:, o_fk:o_fv].reshape(bs, ts, fox_heads, hd))
        vs_s.append(zs[:, o_fv:n_main].reshape(bs, ts, fox_heads, hd))
        lfs_s.append(unpad(lf_s)[:, :fox_heads].reshape(bs, ts, fox_heads))

        i = l // 2
        if l % 2 == 0:
            a = ffn_up_ws(hb2, ffn_w_gate, ffn_w_up, i, PROMPT_WS_TM)
            y = ffn_down(a, ffn_w_down, i, PROMPT_TM)
            if next_in is None:
                xp = ln_residual(x1p, y, g2p, ln2_w[l], ln2_b[l], alpha, PROMPT_LN_TM, tp)
            else:
                xp, hb = ln_residual(x1p, y, g2p, ln2_w[l], ln2_b[l], alpha, PROMPT_LN_TM, tp, next_mod=next_in)
            a_s = ffn_up(x1s, sc2s, sh2s, ffn_w_gate, ffn_w_up, i, ns_rows, ts)
            y_s = ffn_down(a_s, ffn_w_down, i, ns_rows)
            xs = ln_residual(x1s, y_s, g2s, ln2_w[l], ln2_b[l], alpha, ns_rows, ts)
        else:
            h_p, info_p = moe_route(x1p, sc2p, sh2p, router_w[i], router_b[i], n_experts, PROMPT_LN_TM, tp)
            h_s, info_s = moe_route(x1s, sc2s, sh2s, router_w[i], router_b[i], n_experts, ns_rows, ts)
            experts = jnp.concatenate([info_p[:, :TOP_K], info_s[:, :TOP_K]], axis=0).astype(jnp.int32)
            dest, src_token, tile_expert, n_used = moe_plan(experts, n_experts, MOE_TM)
            x_sorted = moe_gather(jnp.concatenate([h_p, h_s], axis=0), src_token, n_used, MOE_TM)
            a = moe_up(x_sorted, moe_w_gate, moe_w_up, i, tile_expert, n_used, MOE_TM)
            y = moe_down(a, moe_w_down, i, tile_expert, n_used, MOE_TM)
            xp = moe_combine_ln(y, dest, info_p, x1p, g2p, ln2_w[l], ln2_b[l], 0, alpha, MOE_COMBINE_TM, tp)
            xs = moe_combine_ln(y, dest, info_s, x1s, g2s, ln2_w[l], ln2_b[l], np_rows, alpha, ns_rows, ts)
            if next_in is not None:
                hb = modulate_cast(xp, next_in[0], next_in[1], PROMPT_LN_TM, tp)

    st = lambda lst: jnp.stack(lst, axis=1)
    k_hm, v_hm = kv_pack(zs_p, o_fk, o_fv, fox_heads, tp)
    k_p = jnp.transpose(k_hm, (0, 1, 3, 2, 4))
    v_p = jnp.transpose(v_hm, (0, 1, 3, 2, 4))
    return (xp.reshape(bp, tp, d), xs.reshape(bs, ts, d), st(rets_p), st(rets_s), st(convs_p), st(convs_s),
            k_p, st(ks_s), v_p, st(vs_s), st(lfs_p), st(lfs_s))
```
